```python
import math
import jax, jax.numpy as jnp
from jax import lax
import numpy as np

D_MODEL = 2048
BATCH = 4
SEQ = 4096
DEPTH = 4

CHUNK = 64
Q_BLOCK = 128
N_MIXERS = 2
N_A = (DEPTH + 1) // 2
N_B = DEPTH // 2

SB_HEADS = 16
SB_HEAD_DIM = D_MODEL // SB_HEADS
SB_IN = 3 * SB_HEADS * SB_HEAD_DIM

DSA_HEADS = 16
DSA_KV_HEADS = 4
DSA_GROUP = DSA_HEADS // DSA_KV_HEADS
DSA_HEAD_DIM = D_MODEL // DSA_HEADS
IDX_HEADS = 16
IDX_HEAD_DIM = 64
IDX_TOPK_MAX = 256
DSA_SIZES = (DSA_HEADS * DSA_HEAD_DIM,
             DSA_KV_HEADS * DSA_HEAD_DIM,
             DSA_KV_HEADS * DSA_HEAD_DIM,
             IDX_HEADS * IDX_HEAD_DIM,
             IDX_HEAD_DIM,
             IDX_HEADS)
DSA_IN = sum(DSA_SIZES)

REL_BUCKETS = 32
REL_MAX_DIST = 128

D_FF = 4 * D_MODEL
RMS_EPS = 1e-6

kernel_name = "hybrid_stickbreak_dsa_streaming_trunk"


def rms_norm(x, g):
    xf = x.astype(jnp.float32)
    y = xf * lax.rsqrt(jnp.mean(xf * xf, axis=-1, keepdims=True) + RMS_EPS)
    return (y * g.astype(jnp.float32)).astype(x.dtype)


def t5_bucket(rel):
    half = REL_BUCKETS // 2
    max_exact = half // 2
    n = jnp.abs(rel)
    nf = jnp.maximum(n, max_exact).astype(jnp.float32)
    large = max_exact + (jnp.log(nf / max_exact) / math.log(REL_MAX_DIST / max_exact)
                         * (half - max_exact)).astype(jnp.int32)
    large = jnp.minimum(large, half - 1)
    return jnp.where(rel > 0, half, 0) + jnp.where(n < max_exact, n, large)


def stick_breaking_mixer(h, w_in, w_out):
    b, s, _ = h.shape
    q, k, v = jnp.split(h @ w_in, 3, axis=-1)
    q = q.reshape(b, s, SB_HEADS, SB_HEAD_DIM)
    k = k.reshape(b, s, SB_HEADS, SB_HEAD_DIM)
    v = v.reshape(b, s, SB_HEADS, SB_HEAD_DIM)
    scale = SB_HEAD_DIM ** -0.5
    outs = []
    for i in range(s // Q_BLOCK):
        q0 = i * Q_BLOCK
        q1 = q0 + Q_BLOCK
        z = jnp.einsum('bqhd,bkhd->bhqk', q[:, q0:q1], k[:, :q1],
                       preferred_element_type=jnp.float32) * scale
        t_pos = q0 + jnp.arange(Q_BLOCK)[:, None]
        s_pos = jnp.arange(q1)[None, :]
        causal = s_pos < t_pos
        log_1mb = jnp.where(causal, jax.nn.log_sigmoid(-z), 0.0)
        tail = lax.cumsum(log_1mb, axis=3, reverse=True) - log_1mb
        a = jnp.where(causal, jnp.exp(jax.nn.log_sigmoid(z) + tail), 0.0)
        outs.append(jnp.einsum('bhqk,bkhd->bqhd', a.astype(v.dtype), v[:, :q1]))
    o = jnp.concatenate(outs, axis=1).reshape(b, s, SB_HEADS * SB_HEAD_DIM)
    return o @ w_out


def dsa_mixer(h, w_in, w_out, q_gain, k_gain, rel_bias):
    b, s, _ = h.shape
    topk = min(IDX_TOPK_MAX, s // 4)
    splits = np.cumsum(DSA_SIZES)[:-1].tolist()
    q, k, v, qi, ki, wi = jnp.split(h @ w_in, splits, axis=-1)
    q = rms_norm(q.reshape(b, s, DSA_HEADS, DSA_HEAD_DIM), q_gain)
    k = rms_norm(k.reshape(b, s, DSA_KV_HEADS, DSA_HEAD_DIM), k_gain)
    v = v.reshape(b, s, DSA_KV_HEADS, DSA_HEAD_DIM)
    qi = qi.reshape(b, s, IDX_HEADS, IDX_HEAD_DIM)
    wi = wi.astype(jnp.float32) * IDX_HEADS ** -0.5
    idx_scale = IDX_HEAD_DIM ** -0.5
    att_scale = DSA_HEAD_DIM ** -0.5
    gather = jax.vmap(lambda arr, ind: arr[ind])
    outs = []
    for i in range(s // Q_BLOCK):
        q0 = i * Q_BLOCK
        q1 = q0 + Q_BLOCK
        t_pos = q0 + jnp.arange(Q_BLOCK)
        chunk_end = (t_pos // CHUNK + 1) * CHUNK
        admiss = jnp.arange(q1)[None, :] < chunk_end[:, None]
        dots = jnp.einsum('bqhd,bkd->bqhk', qi[:, q0:q1], ki[:, :q1],
                          preferred_element_type=jnp.float32) * idx_scale
        score = jnp.einsum('bqh,bqhk->bqk', wi[:, q0:q1], jax.nn.relu(dots))
        score = jnp.where(admiss[None], score, -jnp.inf)
        kb = min(topk, q1)
        _, sel = lax.top_k(score, kb)
        valid = sel < chunk_end[None, :, None]
        k_sel = gather(k[:, :q1], sel)
        v_sel = gather(v[:, :q1], sel)
        qg = q[:, q0:q1].reshape(b, Q_BLOCK, DSA_KV_HEADS, DSA_GROUP, DSA_HEAD_DIM)
        logits = jnp.einsum('bqngd,bqjnd->bqngj', qg, k_sel,
                            preferred_element_type=jnp.float32) * att_scale
        bias = rel_bias[:, t5_bucket(sel - t_pos[None, :, None])]
        bias = bias.reshape(DSA_KV_HEADS, DSA_GROUP, b, Q_BLOCK, kb).transpose(2, 3, 0, 1, 4)
        logits = jnp.where(valid[:, :, None, None, :], logits + bias.astype(jnp.float32), -jnp.inf)
        p = jax.nn.softmax(logits, axis=-1)
        o = jnp.einsum('bqngj,bqjnd->bqngd', p.astype(v.dtype), v_sel)
        outs.append(o.reshape(b, Q_BLOCK, DSA_HEADS * DSA_HEAD_DIM))
    o = jnp.concatenate(outs, axis=1)
    return o @ w_out


def setup_inputs(seed: int = 0) -> dict:
    key = jax.random.key(seed)
    ks = jax.random.split(key, 13)
    f32 = jnp.float32
    d_in = D_MODEL ** -0.5
    return {
        "x": jax.random.normal(ks[0], (BATCH, SEQ, D_MODEL), f32),
        "norm_mix": 1.0 + 0.02 * jax.random.normal(ks[1], (DEPTH, D_MODEL), f32),
        "w_in_a": jax.random.normal(ks[2], (N_A, D_MODEL, SB_IN), f32) * d_in,
        "w_out_a": jax.random.normal(ks[3], (N_A, SB_HEADS * SB_HEAD_DIM, D_MODEL), f32) * (SB_HEADS * SB_HEAD_DIM) ** -0.5,
        "w_in_b": jax.random.normal(ks[4], (N_B, D_MODEL, DSA_IN), f32) * d_in,
        "w_out_b": jax.random.normal(ks[5], (N_B, DSA_HEADS * DSA_HEAD_DIM, D_MODEL), f32) * (DSA_HEADS * DSA_HEAD_DIM) ** -0.5,
        "q_norm_b": 1.0 + 0.02 * jax.random.normal(ks[6], (N_B, DSA_HEAD_DIM), f32),
        "k_norm_b": 1.0 + 0.02 * jax.random.normal(ks[7], (N_B, DSA_HEAD_DIM), f32),
        "rel_bias": 0.5 * jax.random.normal(ks[8], (DSA_HEADS, REL_BUCKETS), f32),
        "norm_mlp": 1.0 + 0.02 * jax.random.normal(ks[9], (DEPTH, D_MODEL), f32),
        "w_up": jax.random.normal(ks[10], (DEPTH, D_MODEL, D_FF), f32) * d_in,
        "w_down": jax.random.normal(ks[11], (DEPTH, D_FF, D_MODEL), f32) * D_FF ** -0.5,
    }


def reference(x, norm_mix, w_in_a, w_out_a, w_in_b, w_out_b, q_norm_b, k_norm_b,
              rel_bias, norm_mlp, w_up, w_down):
    for i in range(DEPTH):
        h = rms_norm(x, norm_mix[i])
        j = i // N_MIXERS
        if i % N_MIXERS == 0:
            x = x + stick_breaking_mixer(h, w_in_a[j], w_out_a[j])
        else:
            x = x + dsa_mixer(h, w_in_b[j], w_out_b[j], q_norm_b[j], k_norm_b[j], rel_bias)
        h = rms_norm(x, norm_mlp[i])
        x = x + jnp.square(jax.nn.relu(h @ w_up[i])) @ w_down[i]
    return x
```

```python
import functools
import math

import jax
import jax.numpy as jnp
import numpy as np
from jax import lax
from jax.experimental import pallas as pl
from jax.experimental.pallas import tpu as pltpu

F32 = jnp.float32
BF16 = jnp.bfloat16

RMS_EPS = 1e-6
CHUNK = 64
BLK = 128
HEAD_DIM = 128
SB_HEADS = 16
DSA_HEADS = 16
DSA_KV_HEADS = 4
DSA_GROUP = DSA_HEADS // DSA_KV_HEADS
IDX_HEADS = 16
IDX_HEAD_DIM = 64
IDX_TOPK_MAX = 256
REL_BUCKETS = 32
REL_MAX_DIST = 128
MASKED = -1e30
MAX_BISECT = 64

_NT = (((1,), (1,)), ((), ()))


def _rms_rows(x, g):
    return x * lax.rsqrt(jnp.mean(x * x, axis=-1, keepdims=True) + RMS_EPS) * g


def _norm_matmul_kernel(x_ref, g_ref, w_ref, o_ref, xn_ref):
    @pl.when(pl.program_id(1) == 0)
    def _():
        xn_ref[...] = _rms_rows(x_ref[...], g_ref[...]).astype(BF16)

    o_ref[...] = jnp.dot(xn_ref[...], w_ref[...], preferred_element_type=F32).astype(o_ref.dtype)


def norm_matmul(x, g, w, out_dtype, tm=512, tn=512):
    n, d = x.shape
    m = w.shape[1]
    return pl.pallas_call(
        _norm_matmul_kernel,
        grid=(n // tm, m // tn),
        in_specs=[pl.BlockSpec((tm, d), lambda i, j: (i, 0)),
                  pl.BlockSpec((1, d), lambda i, j: (0, 0)),
                  pl.BlockSpec((d, tn), lambda i, j: (0, j))],
        out_specs=pl.BlockSpec((tm, tn), lambda i, j: (i, j)),
        out_shape=jax.ShapeDtypeStruct((n, m), out_dtype),
        scratch_shapes=[pltpu.VMEM((tm, d), BF16)],
        compiler_params=pltpu.CompilerParams(dimension_semantics=("parallel", "arbitrary")),
        name="norm_matmul",
    )(x, g.reshape(1, d), w)


def _matmul_residual_kernel(a_ref, w_ref, r_ref, o_ref):
    o_ref[...] = r_ref[...] + jnp.dot(a_ref[...], w_ref[...], preferred_element_type=F32)


def matmul_residual(a, w, res, tm=512, tn=512):
    n, k = a.shape
    m = w.shape[1]
    return pl.pallas_call(
        _matmul_residual_kernel,
        grid=(n // tm, m // tn),
        in_specs=[pl.BlockSpec((tm, k), lambda i, j: (i, 0)),
                  pl.BlockSpec((k, tn), lambda i, j: (0, j)),
                  pl.BlockSpec((tm, tn), lambda i, j: (i, j))],
        out_specs=pl.BlockSpec((tm, tn), lambda i, j: (i, j)),
        out_shape=jax.ShapeDtypeStruct((n, m), F32),
        compiler_params=pltpu.CompilerParams(dimension_semantics=("parallel", "parallel")),
        name="matmul_residual",
    )(a, w, res)


def _mlp_kernel(x_ref, g_ref, wu_ref, wd_ref, o_ref, xn_ref):
    @pl.when(pl.program_id(1) == 0)
    def _():
        x = x_ref[...]
        xn_ref[...] = _rms_rows(x, g_ref[...]).astype(BF16)
        o_ref[...] = x

    h = jnp.dot(xn_ref[...], wu_ref[...], preferred_element_type=F32)
    a = jnp.square(jnp.maximum(h, 0.0)).astype(BF16)
    o_ref[...] += jnp.dot(a, wd_ref[...], preferred_element_type=F32)


def mlp_residual(x, g, w_up, w_down, tm=512, tf=512):
    n, d = x.shape
    dff = w_up.shape[1]
    return pl.pallas_call(
        _mlp_kernel,
        grid=(n // tm, dff // tf),
        in_specs=[pl.BlockSpec((tm, d), lambda i, f: (i, 0)),
                  pl.BlockSpec((1, d), lambda i, f: (0, 0)),
                  pl.BlockSpec((d, tf), lambda i, f: (0, f)),
                  pl.BlockSpec((tf, d), lambda i, f: (f, 0))],
        out_specs=pl.BlockSpec((tm, d), lambda i, f: (i, 0)),
        out_shape=jax.ShapeDtypeStruct((n, d), F32),
        scratch_shapes=[pltpu.VMEM((tm, d), BF16)],
        compiler_params=pltpu.CompilerParams(dimension_semantics=("parallel", "arbitrary")),
        name="mlp_residual",
    )(x, g.reshape(1, d), w_up, w_down)


def _stick_breaking_kernel(q_ref, k_ref, v_ref, o_ref):
    i = pl.program_id(2)
    q = q_ref[...]
    scale = HEAD_DIM ** -0.5
    row = lax.broadcasted_iota(jnp.int32, (BLK, BLK), 0)
    col = lax.broadcasted_iota(jnp.int32, (BLK, BLK), 1)
    later = (row > col).astype(BF16)
    causal = col < row

    def block(j, carry, acc, diagonal):
        start = pl.multiple_of(j * BLK, BLK)
        kj = k_ref[pl.ds(start, BLK), :]
        vj = v_ref[pl.ds(start, BLK), :]
        z = lax.dot_general(q, kj, _NT, preferred_element_type=F32) * scale
        log_b = jnp.minimum(z, 0.0) - jnp.log1p(jnp.exp(-jnp.abs(z)))
        log_1mb = log_b - z
        if diagonal:
            log_1mb = jnp.where(causal, log_1mb, 0.0)
        hi = log_1mb.astype(BF16)
        lo = (log_1mb - hi.astype(F32)).astype(BF16)
        tail = (jnp.dot(hi, later, preferred_element_type=F32)
                + jnp.dot(lo, later, preferred_element_type=F32) + carry)
        a = jnp.exp(log_b + tail)
        if diagonal:
            a = jnp.where(causal, a, 0.0)
        acc = acc + jnp.dot(a.astype(BF16), vj, preferred_element_type=F32)
        carry = carry + jnp.sum(log_1mb, axis=1, keepdims=True)
        return carry, acc

    carry, acc = block(i, jnp.zeros((BLK, 1), F32), jnp.zeros((BLK, HEAD_DIM), F32), True)

    def body(step, state):
        return block(i - 1 - step, state[0], state[1], False)

    carry, acc = lax.fori_loop(0, i, body, (carry, acc))
    o_ref[...] = acc.astype(o_ref.dtype)


def stick_breaking_attention(qkv):
    b, s, _ = qkv.shape
    h = SB_HEADS
    return pl.pallas_call(
        _stick_breaking_kernel,
        grid=(b, h, s // BLK),
        in_specs=[pl.BlockSpec((None, BLK, HEAD_DIM), lambda bi, hi, i: (bi, i, hi)),
                  pl.BlockSpec((None, s, HEAD_DIM), lambda bi, hi, i: (bi, 0, h + hi)),
                  pl.BlockSpec((None, s, HEAD_DIM), lambda bi, hi, i: (bi, 0, 2 * h + hi))],
        out_specs=pl.BlockSpec((None, BLK, HEAD_DIM), lambda bi, hi, i: (bi, i, hi)),
        out_shape=jax.ShapeDtypeStruct((b, s, h * HEAD_DIM), BF16),
        compiler_params=pltpu.CompilerParams(
            dimension_semantics=("parallel", "parallel", "arbitrary")),
        name="stick_breaking_attention",
    )(qkv, qkv, qkv)


DSA_Q = DSA_HEADS * HEAD_DIM
DSA_KV = DSA_KV_HEADS * HEAD_DIM
DSA_QI = IDX_HEADS * IDX_HEAD_DIM
DSA_SPLITS = (0, DSA_Q, DSA_Q + DSA_KV, DSA_Q + 2 * DSA_KV, DSA_Q + 2 * DSA_KV + DSA_QI,
              DSA_Q + 2 * DSA_KV + DSA_QI + IDX_HEAD_DIM,
              DSA_Q + 2 * DSA_KV + DSA_QI + IDX_HEAD_DIM + IDX_HEADS)
DSA_IN = DSA_SPLITS[-1]
DSA_IN_PADDED = -(-DSA_IN // 128) * 128


def _dsa_prep_kernel(p_ref, qg_ref, kg_ref, q_ref, k_ref, v_ref, qi_ref, ki_ref, wi_ref):
    s = DSA_SPLITS
    for h in range(DSA_HEADS):
        lo = s[0] + h * HEAD_DIM
        q_ref[:, h * HEAD_DIM:(h + 1) * HEAD_DIM] = _rms_rows(
            p_ref[:, lo:lo + HEAD_DIM], qg_ref[...]).astype(BF16)
    for h in range(DSA_KV_HEADS):
        lo = s[1] + h * HEAD_DIM
        k_ref[:, h * HEAD_DIM:(h + 1) * HEAD_DIM] = _rms_rows(
            p_ref[:, lo:lo + HEAD_DIM], kg_ref[...]).astype(BF16)
    v_ref[...] = p_ref[:, s[2]:s[3]].astype(BF16)
    qi_ref[...] = p_ref[:, s[3]:s[4]].astype(BF16)
    ki_ref[...] = p_ref[:, s[4]:s[5]].astype(BF16)
    wi_ref[...] = p_ref[:, s[5]:s[6]] * IDX_HEADS ** -0.5


def dsa_prep(proj, q_gain, k_gain, tm=256):
    n = proj.shape[0]
    widths = (DSA_Q, DSA_KV, DSA_KV, DSA_QI, IDX_HEAD_DIM, IDX_HEADS)
    dtypes = (BF16, BF16, BF16, BF16, BF16, F32)
    return pl.pallas_call(
        _dsa_prep_kernel,
        grid=(n // tm,),
        in_specs=[pl.BlockSpec((tm, DSA_IN_PADDED), lambda i: (i, 0)),
                  pl.BlockSpec((1, HEAD_DIM), lambda i: (0, 0)),
                  pl.BlockSpec((1, HEAD_DIM), lambda i: (0, 0))],
        out_specs=[pl.BlockSpec((tm, w), lambda i: (i, 0)) for w in widths],
        out_shape=[jax.ShapeDtypeStruct((n, w), t) for w, t in zip(widths, dtypes)],
        compiler_params=pltpu.CompilerParams(dimension_semantics=("parallel",)),
        name="dsa_prep",
    )(proj, q_gain.reshape(1, HEAD_DIM), k_gain.reshape(1, HEAD_DIM))


def _t5_bucket_tiles():
    half = REL_BUCKETS // 2
    max_exact = half // 2
    t = np.arange(BLK)[:, None]
    s = np.arange(BLK)[None, :]
    tiles = []
    for behind in range(3):
        rel = s - t - behind * BLK
        n = np.abs(rel)
        nf = np.maximum(n, max_exact).astype(np.float64)
        large = max_exact + (np.log(nf / max_exact) / math.log(REL_MAX_DIST / max_exact)
                             * (half - max_exact)).astype(np.int32)
        large = np.minimum(large, half - 1)
        tiles.append(np.where(rel > 0, half, 0) + np.where(n < max_exact, n, large))
    return np.stack(tiles).astype(np.int32)


def _rel_bias_tiles_kernel(rb_ref, bucket_ref, o_ref):
    h = pl.program_id(1)
    bucket = bucket_ref[...]
    acc = jnp.zeros((BLK, BLK), F32)
    for b in range(REL_BUCKETS):
        acc = jnp.where(bucket == b, rb_ref[h, b], acc)
    o_ref[...] = acc


def rel_bias_tiles(rel_bias):
    return pl.pallas_call(
        _rel_bias_tiles_kernel,
        grid=(3, DSA_HEADS),
        in_specs=[pl.BlockSpec(memory_space=pltpu.SMEM),
                  pl.BlockSpec((None, BLK, BLK), lambda d, h: (d, 0, 0))],
        out_specs=pl.BlockSpec((None, BLK, BLK), lambda d, h: (d, h, 0)),
        out_shape=jax.ShapeDtypeStruct((3, DSA_HEADS * BLK, BLK), F32),
        name="rel_bias_tiles",
    )(rel_bias, jnp.asarray(_t5_bucket_tiles()))


def _dsa_attention_kernel(topk, q_ref, k_ref, v_ref, qi_ref, ki_ref, wi_ref, bias_ref,
                          o_ref, score_ref, sel_ref):
    i = pl.program_id(1)
    nblk = i + 1
    row = lax.broadcasted_iota(jnp.int32, (BLK, BLK), 0)
    col = lax.broadcasted_iota(jnp.int32, (BLK, BLK), 1)
    admissible_diag = col < (row // CHUNK + 1) * CHUNK

    wi = wi_ref[...]

    def score_block(j, _):
        start = pl.multiple_of(j * BLK, BLK)
        kij = ki_ref[pl.ds(start, BLK), :]
        s = jnp.zeros((BLK, BLK), F32)
        for h in range(IDX_HEADS):
            d = lax.dot_general(qi_ref[:, h * IDX_HEAD_DIM:(h + 1) * IDX_HEAD_DIM], kij, _NT,
                                preferred_element_type=F32) * IDX_HEAD_DIM ** -0.5
            s = s + wi[:, h:h + 1] * jnp.maximum(d, 0.0)
        s = jnp.where((j < i) | admissible_diag, s, -jnp.inf)
        score_ref[:, pl.ds(start, BLK)] = s
        return 0

    lax.fori_loop(0, nblk, score_block, 0)

    t_local = lax.broadcasted_iota(jnp.int32, (BLK, 1), 0)
    n_adm = i * BLK + (t_local // CHUNK + 1) * CHUNK
    keep = jnp.minimum(jnp.minimum(topk, nblk * BLK), n_adm).astype(F32)

    def count_ge(x):
        xb = jnp.broadcast_to(x, (BLK, BLK))

        def body(j, c):
            start = pl.multiple_of(j * BLK, BLK)
            return c + (score_ref[:, pl.ds(start, BLK)] >= xb).astype(F32)

        c = lax.fori_loop(0, nblk, body, jnp.zeros((BLK, BLK), F32))
        return jnp.sum(c, axis=1, keepdims=True)

    def minmax_body(j, st):
        start = pl.multiple_of(j * BLK, BLK)
        s = score_ref[:, pl.ds(start, BLK)]
        return (jnp.minimum(st[0], jnp.where(s == -jnp.inf, jnp.inf, s)), jnp.maximum(st[1], s))

    mn, mx = lax.fori_loop(0, nblk, minmax_body,
                           (jnp.full((BLK, BLK), jnp.inf, F32), jnp.full((BLK, BLK), -jnp.inf, F32)))
    lo0 = jnp.min(mn, axis=1, keepdims=True)
    hi0 = jnp.max(mx, axis=1, keepdims=True)

    def bisect_cond(st):
        it, _, _, clo = st
        return (it < MAX_BISECT) & (jnp.max(clo - keep) > 0.0)

    def bisect_body(st):
        it, lo, hi, clo = st
        mid = 0.5 * lo + 0.5 * hi
        c = count_ge(mid)
        ge = c >= keep
        return (it + 1, jnp.where(ge, mid, lo), jnp.where(ge, hi, mid), jnp.where(ge, c, clo))

    _, thr, _, _ = lax.while_loop(bisect_cond, bisect_body,
                                  (jnp.int32(0), lo0, hi0, n_adm.astype(F32)))

    def select_block(j, _):
        start = pl.multiple_of(j * BLK, BLK)
        sel_ref[:, pl.ds(start, BLK)] = jnp.where(score_ref[:, pl.ds(start, BLK)] >= thr, 0.0, MASKED)
        return 0

    lax.fori_loop(0, nblk, select_block, 0)

    rows = DSA_GROUP * BLK
    for n in range(DSA_KV_HEADS):
        qn = jnp.concatenate(
            [q_ref[:, (n * DSA_GROUP + g) * HEAD_DIM:(n * DSA_GROUP + g + 1) * HEAD_DIM]
             for g in range(DSA_GROUP)], axis=0)

        def attend(j, st, n=n, qn=qn):
            m, l, acc = st
            start = pl.multiple_of(j * BLK, BLK)
            kj = k_ref[pl.ds(start, BLK), n * HEAD_DIM:(n + 1) * HEAD_DIM]
            vj = v_ref[pl.ds(start, BLK), n * HEAD_DIM:(n + 1) * HEAD_DIM]
            s = lax.dot_general(qn, kj, _NT, preferred_element_type=F32) * HEAD_DIM ** -0.5
            sel = sel_ref[:, pl.ds(start, BLK)]
            s = s + bias_ref[jnp.minimum(i - j, 2), n * rows:(n + 1) * rows, :]
            s = s + jnp.concatenate([sel] * DSA_GROUP, axis=0)
            m_new = jnp.maximum(m, jnp.max(s, axis=1, keepdims=True))
            alpha = jnp.exp(m - m_new)
            p = jnp.exp(s - m_new)
            l = alpha * l + jnp.sum(p, axis=1, keepdims=True)
            acc = alpha * acc + jnp.dot(p.astype(BF16), vj, preferred_element_type=F32)
            return m_new, l, acc

        m, l, acc = lax.fori_loop(
            0, nblk, attend,
            (jnp.full((rows, 1), MASKED, F32), jnp.zeros((rows, 1), F32), jnp.zeros((rows, HEAD_DIM), F32)))
        o = acc / l
        for g in range(DSA_GROUP):
            h = n * DSA_GROUP + g
            o_ref[:, h * HEAD_DIM:(h + 1) * HEAD_DIM] = o[g * BLK:(g + 1) * BLK].astype(o_ref.dtype)


def dsa_attention(q, k, v, qi, ki, wi, bias_tiles):
    b, s, _ = q.shape
    topk = min(IDX_TOPK_MAX, s // 4)
    return pl.pallas_call(
        functools.partial(_dsa_attention_kernel, topk),
        grid=(b, s // BLK),
        in_specs=[pl.BlockSpec((None, BLK, DSA_Q), lambda bi, i: (bi, i, 0)),
                  pl.BlockSpec((None, s, DSA_KV), lambda bi, i: (bi, 0, 0)),
                  pl.BlockSpec((None, s, DSA_KV), lambda bi, i: (bi, 0, 0)),
                  pl.BlockSpec((None, BLK, DSA_QI), lambda bi, i: (bi, i, 0)),
                  pl.BlockSpec((None, s, IDX_HEAD_DIM), lambda bi, i: (bi, 0, 0)),
                  pl.BlockSpec((None, BLK, IDX_HEADS), lambda bi, i: (bi, i, 0)),
                  pl.BlockSpec((3, DSA_HEADS * BLK, BLK), lambda bi, i: (0, 0, 0))],
        out_specs=pl.BlockSpec((None, BLK, DSA_Q), lambda bi, i: (bi, i, 0)),
        out_shape=jax.ShapeDtypeStruct((b, s, DSA_Q), BF16),
        scratch_shapes=[pltpu.VMEM((BLK, s), F32), pltpu.VMEM((BLK, s), F32)],
        compiler_params=pltpu.CompilerParams(dimension_semantics=("parallel", "arbitrary")),
        name="dsa_attention",
    )(q, k, v, qi, ki, wi, bias_tiles)


def kernel(x, norm_mix, w_in_a, w_out_a, w_in_b, w_out_b, q_norm_b, k_norm_b, rel_bias,
           norm_mlp, w_up, w_down):
    b, s, d = x.shape
    n = b * s
    depth = norm_mix.shape[0]
    w_in_b = jnp.pad(w_in_b, ((0, 0), (0, 0), (0, DSA_IN_PADDED - DSA_IN)))
    w_in_a, w_out_a, w_in_b, w_out_b, w_up, w_down = (
        w.astype(BF16) for w in (w_in_a, w_out_a, w_in_b, w_out_b, w_up, w_down))
    bias_tiles = rel_bias_tiles(rel_bias)

    x = x.reshape(n, d)
    for layer in range(depth):
        j = layer // 2
        if layer % 2 == 0:
            qkv = norm_matmul(x, norm_mix[layer], w_in_a[j], BF16)
            o = stick_breaking_attention(qkv.reshape(b, s, -1))
            x = matmul_residual(o.reshape(n, -1), w_out_a[j], x)
        else:
            proj = norm_matmul(x, norm_mix[layer], w_in_b[j], F32, tn=DSA_IN_PADDED // 11)
            parts = dsa_prep(proj, q_norm_b[j], k_norm_b[j])
            q, k, v, qi, ki, wi = (p.reshape(b, s, -1) for p in parts)
            o = dsa_attention(q, k, v, qi, ki, wi, bias_tiles)
            x = matmul_residual(o.reshape(n, -1), w_out_b[j], x)
        x = mlp_residual(x, norm_mlp[layer], w_up[layer], w_down[layer])
    return x.reshape(b, s, d)
```

```python
import functools
import math

import jax
import jax.numpy as jnp
import numpy as np
from jax import lax
from jax.experimental import pallas as pl
from jax.experimental.pallas import tpu as pltpu

F32 = jnp.float32
BF16 = jnp.bfloat16

RMS_EPS = 1e-6
CHUNK = 64
BLK = 128
HEAD_DIM = 128
SB_HEADS = 16
DSA_HEADS = 16
DSA_KV_HEADS = 4
DSA_GROUP = DSA_HEADS // DSA_KV_HEADS
IDX_HEADS = 16
IDX_HEAD_DIM = 64
IDX_TOPK_MAX = 256
REL_BUCKETS = 32
REL_MAX_DIST = 128
MASKED = -1e30
MAX_BISECT = 320
BISECT_STEPS_PER_CHECK = 4
EXP_IS_ZERO_BELOW = -104.0

_NT = (((1,), (1,)), ((), ()))


def _rms_rows(x, g):
    return x * lax.rsqrt(jnp.mean(x * x, axis=-1, keepdims=True) + RMS_EPS) * g


def _norm_matmul_kernel(x_ref, g_ref, w_ref, o_ref, xn_ref):
    @pl.when(pl.program_id(1) == 0)
    def _():
        xn_ref[...] = _rms_rows(x_ref[...], g_ref[...]).astype(BF16)

    o_ref[...] = jnp.dot(xn_ref[...], w_ref[...], preferred_element_type=F32).astype(o_ref.dtype)


def norm_matmul(x, g, w, out_dtype, tm=512, tn=512):
    n, d = x.shape
    m = w.shape[1]
    return pl.pallas_call(
        _norm_matmul_kernel,
        grid=(n // tm, m // tn),
        in_specs=[pl.BlockSpec((tm, d), lambda i, j: (i, 0)),
                  pl.BlockSpec((1, d), lambda i, j: (0, 0)),
                  pl.BlockSpec((d, tn), lambda i, j: (0, j))],
        out_specs=pl.BlockSpec((tm, tn), lambda i, j: (i, j)),
        out_shape=jax.ShapeDtypeStruct((n, m), out_dtype),
        scratch_shapes=[pltpu.VMEM((tm, d), BF16)],
        compiler_params=pltpu.CompilerParams(dimension_semantics=("parallel", "arbitrary")),
        name="norm_matmul",
    )(x, g.reshape(1, d), w)


def _matmul_residual_kernel(a_ref, w_ref, r_ref, o_ref):
    o_ref[...] = r_ref[...] + jnp.dot(a_ref[...], w_ref[...], preferred_element_type=F32)


def matmul_residual(a, w, res, tm=512, tn=512):
    n, k = a.shape
    m = w.shape[1]
    return pl.pallas_call(
        _matmul_residual_kernel,
        grid=(n // tm, m // tn),
        in_specs=[pl.BlockSpec((tm, k), lambda i, j: (i, 0)),
                  pl.BlockSpec((k, tn), lambda i, j: (0, j)),
                  pl.BlockSpec((tm, tn), lambda i, j: (i, j))],
        out_specs=pl.BlockSpec((tm, tn), lambda i, j: (i, j)),
        out_shape=jax.ShapeDtypeStruct((n, m), F32),
        compiler_params=pltpu.CompilerParams(dimension_semantics=("parallel", "parallel")),
        name="matmul_residual",
    )(a, w, res)


def _mlp_kernel(x_ref, g_ref, wu_ref, wd_ref, o_ref, xn_ref):
    @pl.when(pl.program_id(1) == 0)
    def _():
        x = x_ref[...]
        xn_ref[...] = _rms_rows(x, g_ref[...]).astype(BF16)
        o_ref[...] = x

    h = jnp.dot(xn_ref[...], wu_ref[...], preferred_element_type=F32)
    a = jnp.square(jnp.maximum(h, 0.0)).astype(BF16)
    o_ref[...] += jnp.dot(a, wd_ref[...], preferred_element_type=F32)


def mlp_residual(x, g, w_up, w_down, tm=512, tf=512):
    n, d = x.shape
    dff = w_up.shape[1]
    return pl.pallas_call(
        _mlp_kernel,
        grid=(n // tm, dff // tf),
        in_specs=[pl.BlockSpec((tm, d), lambda i, f: (i, 0)),
                  pl.BlockSpec((1, d), lambda i, f: (0, 0)),
                  pl.BlockSpec((d, tf), lambda i, f: (0, f)),
                  pl.BlockSpec((tf, d), lambda i, f: (f, 0))],
        out_specs=pl.BlockSpec((tm, d), lambda i, f: (i, 0)),
        out_shape=jax.ShapeDtypeStruct((n, d), F32),
        scratch_shapes=[pltpu.VMEM((tm, d), BF16)],
        compiler_params=pltpu.CompilerParams(dimension_semantics=("parallel", "arbitrary")),
        name="mlp_residual",
    )(x, g.reshape(1, d), w_up, w_down)


def _stick_breaking_kernel(q_ref, k_ref, v_ref, o_ref, acc_ref):
    i = pl.program_id(2)
    heads = q_ref.shape[1] // HEAD_DIM
    scale = HEAD_DIM ** -0.5
    row = lax.broadcasted_iota(jnp.int32, (BLK, BLK), 0)
    col = lax.broadcasted_iota(jnp.int32, (BLK, BLK), 1)
    later = (row > col).astype(BF16)
    causal = col < row

    def block(j, carries, diagonal):
        start = pl.multiple_of(j * BLK, BLK)
        out = []
        for g in range(heads):
            cs = slice(g * HEAD_DIM, (g + 1) * HEAD_DIM)
            kj = k_ref[pl.ds(start, BLK), cs]
            vj = v_ref[pl.ds(start, BLK), cs]
            z = lax.dot_general(q_ref[:, cs], kj, _NT, preferred_element_type=F32) * scale
            log_b = jnp.minimum(z, 0.0) - jnp.log1p(jnp.exp(-jnp.abs(z)))
            log_1mb = log_b - z
            if diagonal:
                log_1mb = jnp.where(causal, log_1mb, 0.0)
            hi = log_1mb.astype(BF16)
            lo = (log_1mb - hi.astype(F32)).astype(BF16)
            tail = (jnp.dot(hi, later, preferred_element_type=F32)
                    + jnp.dot(lo, later, preferred_element_type=F32) + carries[g])
            a = jnp.exp(log_b + tail)
            if diagonal:
                a = jnp.where(causal, a, 0.0)
            contrib = jnp.dot(a.astype(BF16), vj, preferred_element_type=F32)
            if diagonal:
                acc_ref[:, cs] = contrib
            else:
                acc_ref[:, cs] += contrib
            out.append(carries[g] + jnp.sum(log_1mb, axis=1, keepdims=True))
        return tuple(out)

    carries = block(i, (jnp.zeros((BLK, 1), F32),) * heads, True)

    def cond(state):
        live = functools.reduce(jnp.maximum, state[1:])
        return (state[0] >= 0) & (jnp.max(live) > EXP_IS_ZERO_BELOW)

    def body(state):
        return (state[0] - 1,) + block(state[0], state[1:], False)

    lax.while_loop(cond, body, (i - 1,) + carries)
    o_ref[...] = acc_ref[...].astype(o_ref.dtype)


def stick_breaking_attention(qkv, heads_per_step=8):
    b, s, _ = qkv.shape
    groups = SB_HEADS // heads_per_step
    w = heads_per_step * HEAD_DIM
    return pl.pallas_call(
        _stick_breaking_kernel,
        grid=(b, groups, s // BLK),
        in_specs=[pl.BlockSpec((None, BLK, w), lambda bi, gi, i: (bi, i, gi)),
                  pl.BlockSpec((None, s, w), lambda bi, gi, i: (bi, 0, groups + gi)),
                  pl.BlockSpec((None, s, w), lambda bi, gi, i: (bi, 0, 2 * groups + gi))],
        out_specs=pl.BlockSpec((None, BLK, w), lambda bi, gi, i: (bi, i, gi)),
        out_shape=jax.ShapeDtypeStruct((b, s, SB_HEADS * HEAD_DIM), BF16),
        scratch_shapes=[pltpu.VMEM((BLK, w), F32)],
        compiler_params=pltpu.CompilerParams(
            dimension_semantics=("parallel", "parallel", "arbitrary")),
        name="stick_breaking_attention",
    )(qkv, qkv, qkv)


DSA_Q = DSA_HEADS * HEAD_DIM
DSA_KV = DSA_KV_HEADS * HEAD_DIM
DSA_QI = IDX_HEADS * IDX_HEAD_DIM
DSA_SPLITS = (0, DSA_Q, DSA_Q + DSA_KV, DSA_Q + 2 * DSA_KV, DSA_Q + 2 * DSA_KV + DSA_QI,
              DSA_Q + 2 * DSA_KV + DSA_QI + IDX_HEAD_DIM,
              DSA_Q + 2 * DSA_KV + DSA_QI + IDX_HEAD_DIM + IDX_HEADS)
DSA_IN = DSA_SPLITS[-1]
DSA_IN_PADDED = -(-DSA_IN // 128) * 128
assert DSA_SPLITS[4] % 128 == 0 and DSA_IN_PADDED - DSA_SPLITS[4] == 128 and IDX_HEAD_DIM % 8 == 0


def _dsa_prep_kernel(p_ref, qg_ref, kg_ref, q_ref, k_ref, vt_ref, qi_ref, ki_ref, wit_ref):
    s = DSA_SPLITS
    for h in range(DSA_HEADS):
        lo = s[0] + h * HEAD_DIM
        q_ref[:, h * HEAD_DIM:(h + 1) * HEAD_DIM] = _rms_rows(
            p_ref[:, lo:lo + HEAD_DIM], qg_ref[...]).astype(BF16)
    for h in range(DSA_KV_HEADS):
        lo = s[1] + h * HEAD_DIM
        k_ref[:, h * HEAD_DIM:(h + 1) * HEAD_DIM] = _rms_rows(
            p_ref[:, lo:lo + HEAD_DIM], kg_ref[...]).astype(BF16)
        lo = s[2] + h * HEAD_DIM
        vt_ref[h * HEAD_DIM:(h + 1) * HEAD_DIM, :] = p_ref[:, lo:lo + HEAD_DIM].T.astype(BF16)
    qi_ref[...] = p_ref[:, s[3]:s[4]].astype(BF16)
    ki_ref[...] = p_ref[:, s[4]:s[5]].astype(BF16)
    tail_t = p_ref[:, s[4]:DSA_IN_PADDED].T
    wit_ref[...] = tail_t[IDX_HEAD_DIM:IDX_HEAD_DIM + IDX_HEADS, :] * (IDX_HEADS ** -0.5 * IDX_HEAD_DIM ** -0.5)


def dsa_prep(proj, q_gain, k_gain):
    n = proj.shape[0]
    nb = n // BLK
    row = lambda w: pl.BlockSpec((BLK, w), lambda i: (i, 0))
    return pl.pallas_call(
        _dsa_prep_kernel,
        grid=(nb,),
        in_specs=[row(DSA_IN_PADDED),
                  pl.BlockSpec((1, HEAD_DIM), lambda i: (0, 0)),
                  pl.BlockSpec((1, HEAD_DIM), lambda i: (0, 0))],
        out_specs=[row(DSA_Q), row(DSA_KV),
                   pl.BlockSpec((None, DSA_KV, BLK), lambda i: (i, 0, 0)),
                   row(DSA_QI), row(IDX_HEAD_DIM),
                   pl.BlockSpec((None, IDX_HEADS, BLK), lambda i: (i, 0, 0))],
        out_shape=[jax.ShapeDtypeStruct((n, DSA_Q), BF16),
                   jax.ShapeDtypeStruct((n, DSA_KV), BF16),
                   jax.ShapeDtypeStruct((nb, DSA_KV, BLK), BF16),
                   jax.ShapeDtypeStruct((n, DSA_QI), BF16),
                   jax.ShapeDtypeStruct((n, IDX_HEAD_DIM), BF16),
                   jax.ShapeDtypeStruct((nb, IDX_HEADS, BLK), F32)],
        compiler_params=pltpu.CompilerParams(dimension_semantics=("parallel",)),
        name="dsa_prep",
    )(proj, q_gain.reshape(1, HEAD_DIM), k_gain.reshape(1, HEAD_DIM))


def _t5_bucket_tiles():
    half = REL_BUCKETS // 2
    max_exact = half // 2
    s = np.arange(BLK)[:, None]
    t = np.arange(BLK)[None, :]
    tiles = []
    for behind in range(3):
        rel = s - t - behind * BLK
        n = np.abs(rel)
        nf = np.maximum(n, max_exact).astype(np.float64)
        large = max_exact + (np.log(nf / max_exact) / math.log(REL_MAX_DIST / max_exact)
                             * (half - max_exact)).astype(np.int32)
        large = np.minimum(large, half - 1)
        tiles.append(np.where(rel > 0, half, 0) + np.where(n < max_exact, n, large))
    return np.stack(tiles).astype(np.int32)


def _rel_bias_tiles_kernel(rb_ref, bucket_ref, o_ref):
    h = pl.program_id(1)
    bucket = bucket_ref[...]
    acc = jnp.zeros((BLK, BLK), F32)
    for b in range(REL_BUCKETS):
        acc = jnp.where(bucket == b, rb_ref[h, b], acc)
    o_ref[...] = acc


def rel_bias_tiles(rel_bias):
    return pl.pallas_call(
        _rel_bias_tiles_kernel,
        grid=(3, DSA_HEADS),
        in_specs=[pl.BlockSpec(memory_space=pltpu.SMEM),
                  pl.BlockSpec((None, BLK, BLK), lambda d, h: (d, 0, 0))],
        out_specs=pl.BlockSpec((None, BLK, BLK), lambda d, h: (d, 0, h)),
        out_shape=jax.ShapeDtypeStruct((3, BLK, DSA_HEADS * BLK), F32),
        name="rel_bias_tiles",
    )(rel_bias, jnp.asarray(_t5_bucket_tiles()))


def _dsa_attention_kernel(topk, q_ref, k_ref, vt_ref, qi_ref, ki_ref, wit_ref, bias_ref,
                          o_ref, score_ref, sel_ref, acc_ref, qn_ref, qi2_ref):
    i = pl.program_id(1)
    nblk = i + 1
    key = lax.broadcasted_iota(jnp.int32, (BLK, BLK), 0)
    qry = lax.broadcasted_iota(jnp.int32, (BLK, BLK), 1)
    admissible_diag = key < (qry // CHUNK + 1) * CHUNK

    for p in range(IDX_HEADS // 2):
        qi2_ref[p] = jnp.concatenate(
            [qi_ref[:, (2 * p + e) * IDX_HEAD_DIM:(2 * p + e + 1) * IDX_HEAD_DIM] for e in range(2)], axis=0)

    def score_block(j, _):
        kij = ki_ref[pl.ds(pl.multiple_of(j * BLK, BLK), BLK), :]
        s = jnp.zeros((BLK, BLK), F32)
        for p in range(IDX_HEADS // 2):
            d = lax.dot_general(kij, qi2_ref[p], _NT, preferred_element_type=F32)
            s = s + jnp.maximum(d[:, :BLK], 0.0) * wit_ref[2 * p:2 * p + 1, :]
            s = s + jnp.maximum(d[:, BLK:], 0.0) * wit_ref[2 * p + 1:2 * p + 2, :]
        score_ref[j] = jnp.where((j < i) | admissible_diag, s, -jnp.inf)
        return 0

    lax.fori_loop(0, nblk, score_block, 0)

    t_local = lax.broadcasted_iota(jnp.int32, (1, BLK), 1)
    n_adm = i * BLK + (t_local // CHUNK + 1) * CHUNK
    keep = jnp.minimum(jnp.minimum(topk, nblk * BLK), n_adm).astype(F32)

    def count(pred, x):
        xb = jnp.broadcast_to(x, (BLK, BLK))
        c = lax.fori_loop(0, nblk, lambda j, c: c + pred(score_ref[j], xb).astype(F32),
                          jnp.zeros((BLK, BLK), F32))
        return jnp.sum(c, axis=0, keepdims=True)

    def minmax_body(j, st):
        s = score_ref[j]
        return (jnp.minimum(st[0], jnp.where(s == -jnp.inf, jnp.inf, s)), jnp.maximum(st[1], s))

    mn, mx = lax.fori_loop(0, nblk, minmax_body,
                           (jnp.full((BLK, BLK), jnp.inf, F32), jnp.full((BLK, BLK), -jnp.inf, F32)))
    lo0 = jnp.min(mn, axis=0, keepdims=True)
    hi0 = jnp.max(mx, axis=0, keepdims=True)

    def midpoint(lo, hi):
        return 0.5 * lo + 0.5 * hi

    def bisect_cond(st):
        it, lo, hi, clo = st
        mid = midpoint(lo, hi)
        open_ = (clo > keep) & (mid > lo) & (mid < hi)
        return (it < MAX_BISECT) & (jnp.max(open_.astype(F32)) > 0.0)

    def bisect_body(st):
        it, lo, hi, clo = st
        for _ in range(BISECT_STEPS_PER_CHECK):
            mid = midpoint(lo, hi)
            c = count(lambda s, x: s >= x, mid)
            ge = c >= keep
            lo, hi, clo = jnp.where(ge, mid, lo), jnp.where(ge, hi, mid), jnp.where(ge, c, clo)
        return it + BISECT_STEPS_PER_CHECK, lo, hi, clo

    _, thr, _, cthr = lax.while_loop(bisect_cond, bisect_body,
                                     (jnp.int32(0), lo0, hi0, n_adm.astype(F32)))
    thr_b = jnp.broadcast_to(thr, (BLK, BLK))
    tied = jnp.max(cthr - keep) > 0.0

    @pl.when(jnp.logical_not(tied))
    def _():
        def select_block(j, _):
            sel_ref[j] = jnp.where(score_ref[j] >= thr_b, 0.0, MASKED)
            return 0

        lax.fori_loop(0, nblk, select_block, 0)

    @pl.when(tied)
    def _():
        room = jnp.broadcast_to(keep - count(lambda s, x: s > x, thr), (BLK, BLK))
        earlier = (qry < key).astype(BF16)

        def select_block(j, seen):
            s = score_ref[j]
            eq = s == thr_b
            rank = jnp.dot(earlier, eq.astype(BF16), preferred_element_type=F32) + seen
            sel_ref[j] = jnp.where((s > thr_b) | (eq & (rank < room)), 0.0, MASKED)
            return seen + jnp.sum(eq.astype(F32), axis=0, keepdims=True)

        lax.fori_loop(0, nblk, select_block, jnp.zeros((1, BLK), F32))

    cols = DSA_GROUP * BLK
    for n in range(DSA_KV_HEADS):
        qn_ref[n] = jnp.concatenate(
            [q_ref[:, (n * DSA_GROUP + g) * HEAD_DIM:(n * DSA_GROUP + g + 1) * HEAD_DIM]
             for g in range(DSA_GROUP)], axis=0)
    acc_ref[...] = jnp.zeros_like(acc_ref)

    def attend(j, st):
        kstart = pl.multiple_of(j * BLK, BLK)
        behind = jnp.minimum(i - j, 2)
        sel = jnp.concatenate([sel_ref[j]] * DSA_GROUP, axis=1)
        out = []
        for n in range(DSA_KV_HEADS):
            m, l = st[2 * n], st[2 * n + 1]
            kj = k_ref[pl.ds(kstart, BLK), n * HEAD_DIM:(n + 1) * HEAD_DIM]
            s = lax.dot_general(kj, qn_ref[n], _NT, preferred_element_type=F32) * HEAD_DIM ** -0.5
            s = s + bias_ref[behind, :, n * cols:(n + 1) * cols] + sel
            m_new = jnp.maximum(m, jnp.max(s, axis=0, keepdims=True))
            alpha = jnp.exp(m - m_new)
            p = jnp.exp(s - m_new)
            acc_ref[n] = alpha * acc_ref[n] + jnp.dot(
                vt_ref[j, n * HEAD_DIM:(n + 1) * HEAD_DIM, :], p.astype(BF16), preferred_element_type=F32)
            out += [m_new, alpha * l + jnp.sum(p, axis=0, keepdims=True)]
        return tuple(out)

    stats = lax.fori_loop(
        0, nblk, attend,
        (jnp.full((1, cols), MASKED, F32), jnp.zeros((1, cols), F32)) * DSA_KV_HEADS)
    for n in range(DSA_KV_HEADS):
        o = acc_ref[n] / stats[2 * n + 1]
        for g in range(DSA_GROUP):
            h = n * DSA_GROUP + g
            o_ref[:, h * HEAD_DIM:(h + 1) * HEAD_DIM] = o[:, g * BLK:(g + 1) * BLK].T.astype(o_ref.dtype)


def dsa_attention(q, k, vt, qi, ki, wit, bias_tiles):
    b, s, _ = q.shape
    nb = s // BLK
    topk = min(IDX_TOPK_MAX, s // 4)
    return pl.pallas_call(
        functools.partial(_dsa_attention_kernel, topk),
        grid=(b, nb),
        in_specs=[pl.BlockSpec((None, BLK, DSA_Q), lambda bi, i: (bi, i, 0)),
                  pl.BlockSpec((None, s, DSA_KV), lambda bi, i: (bi, 0, 0)),
                  pl.BlockSpec((None, nb, DSA_KV, BLK), lambda bi, i: (bi, 0, 0, 0)),
                  pl.BlockSpec((None, BLK, DSA_QI), lambda bi, i: (bi, i, 0)),
                  pl.BlockSpec((None, s, IDX_HEAD_DIM), lambda bi, i: (bi, 0, 0)),
                  pl.BlockSpec((None, None, IDX_HEADS, BLK), lambda bi, i: (bi, i, 0, 0)),
                  pl.BlockSpec((3, BLK, DSA_HEADS * BLK), lambda bi, i: (0, 0, 0))],
        out_specs=pl.BlockSpec((None, BLK, DSA_Q), lambda bi, i: (bi, i, 0)),
        out_shape=jax.ShapeDtypeStruct((b, s, DSA_Q), BF16),
        scratch_shapes=[pltpu.VMEM((nb, BLK, BLK), F32), pltpu.VMEM((nb, BLK, BLK), F32),
                        pltpu.VMEM((DSA_KV_HEADS, HEAD_DIM, DSA_GROUP * BLK), F32),
                        pltpu.VMEM((DSA_KV_HEADS, DSA_GROUP * BLK, HEAD_DIM), BF16),
                        pltpu.VMEM((IDX_HEADS // 2, 2 * BLK, IDX_HEAD_DIM), BF16)],
        compiler_params=pltpu.CompilerParams(dimension_semantics=("parallel", "arbitrary")),
        name="dsa_attention",
    )(q, k, vt, qi, ki, wit, bias_tiles)


def kernel(x, norm_mix, w_in_a, w_out_a, w_in_b, w_out_b, q_norm_b, k_norm_b, rel_bias,
           norm_mlp, w_up, w_down):
    b, s, d = x.shape
    n = b * s
    nb = s // BLK
    depth = norm_mix.shape[0]
    w_in_b = jnp.pad(w_in_b, ((0, 0), (0, 0), (0, DSA_IN_PADDED - DSA_IN)))
    w_in_a, w_out_a, w_in_b, w_out_b, w_up, w_down = (
        w.astype(BF16) for w in (w_in_a, w_out_a, w_in_b, w_out_b, w_up, w_down))
    bias_tiles = rel_bias_tiles(rel_bias)

    x = x.reshape(n, d)
    for layer in range(depth):
        j = layer // 2
        if layer % 2 == 0:
            qkv = norm_matmul(x, norm_mix[layer], w_in_a[j], BF16)
            o = stick_breaking_attention(qkv.reshape(b, s, -1))
            x = matmul_residual(o.reshape(n, -1), w_out_a[j], x)
        else:
            proj = norm_matmul(x, norm_mix[layer], w_in_b[j], F32, tn=DSA_IN_PADDED // 11)
            q, k, vt, qi, ki, wit = dsa_prep(proj, q_norm_b[j], k_norm_b[j])
            o = dsa_attention(q.reshape(b, s, -1), k.reshape(b, s, -1),
                              vt.reshape(b, nb, DSA_KV, BLK), qi.reshape(b, s, -1),
                              ki.reshape(b, s, -1), wit.reshape(b, nb, IDX_HEADS, BLK), bias_tiles)
            x = matmul_residual(o.reshape(n, -1), w_out_b[j], x)
        x = mlp_residual(x, norm_mlp[layer], w_up[layer], w_down[layer])
    return x.reshape(b, s, d)
```

```python
import functools
import math

import jax
import jax.numpy as jnp
import numpy as np
from jax import lax
from jax.experimental import pallas as pl
from jax.experimental.pallas import tpu as pltpu

F32 = jnp.float32
BF16 = jnp.bfloat16

RMS_EPS = 1e-6
NORM_STRIP = 128
CHUNK = 64
BLK = 128
KC = 2 * BLK
HEAD_DIM = 128
SB_HEADS = 16
DSA_HEADS = 16
DSA_KV_HEADS = 4
DSA_GROUP = DSA_HEADS // DSA_KV_HEADS
IDX_HEADS = 16
IDX_HEAD_DIM = 64
IDX_TOPK_MAX = 256
REL_BUCKETS = 32
REL_MAX_DIST = 128
MASKED = -1e30
MAX_BISECT = 320
BISECT_STEPS_PER_CHECK = 4
EXP_IS_ZERO_BELOW = -104.0

_NT = (((1,), (1,)), ((), ()))


def _rms_rows(x, g):
    return x * lax.rsqrt(jnp.mean(x * x, axis=-1, keepdims=True) + RMS_EPS) * g


def _normalize_rows_into(x_ref, g_ref, xn_ref):
    def strip(r, _):
        rows = pl.ds(pl.multiple_of(r * NORM_STRIP, NORM_STRIP), NORM_STRIP)
        xn_ref[rows, :] = _rms_rows(x_ref[rows, :], g_ref[...]).astype(BF16)
        return 0

    lax.fori_loop(0, x_ref.shape[0] // NORM_STRIP, strip, 0)


def _norm_matmul_kernel(x_ref, g_ref, w_ref, o_ref, xn_ref):
    @pl.when(pl.program_id(1) == 0)
    def _():
        _normalize_rows_into(x_ref, g_ref, xn_ref)

    o_ref[...] = jnp.dot(xn_ref[...], w_ref[...], preferred_element_type=F32).astype(o_ref.dtype)


def norm_matmul(x, g, w, out_dtype, tm=1024, tn=1024):
    n, d = x.shape
    m = w.shape[1]
    return pl.pallas_call(
        _norm_matmul_kernel,
        grid=(n // tm, m // tn),
        in_specs=[pl.BlockSpec((tm, d), lambda i, j: (i, 0)),
                  pl.BlockSpec((1, d), lambda i, j: (0, 0)),
                  pl.BlockSpec((d, tn), lambda i, j: (0, j))],
        out_specs=pl.BlockSpec((tm, tn), lambda i, j: (i, j)),
        out_shape=jax.ShapeDtypeStruct((n, m), out_dtype),
        scratch_shapes=[pltpu.VMEM((tm, d), BF16)],
        compiler_params=pltpu.CompilerParams(dimension_semantics=("parallel", "arbitrary")),
        name="norm_matmul",
    )(x, g.reshape(1, d), w)


def _matmul_residual_kernel(a_ref, w_ref, r_ref, o_ref):
    o_ref[...] = r_ref[...] + jnp.dot(a_ref[...], w_ref[...], preferred_element_type=F32)


def matmul_residual(a, w, res, tm=512):
    n, k = a.shape
    m = w.shape[1]
    return pl.pallas_call(
        _matmul_residual_kernel,
        grid=(n // tm,),
        in_specs=[pl.BlockSpec((tm, k), lambda i: (i, 0)),
                  pl.BlockSpec((k, m), lambda i: (0, 0)),
                  pl.BlockSpec((tm, m), lambda i: (i, 0))],
        out_specs=pl.BlockSpec((tm, m), lambda i: (i, 0)),
        out_shape=jax.ShapeDtypeStruct((n, m), F32),
        compiler_params=pltpu.CompilerParams(dimension_semantics=("parallel",)),
        name="matmul_residual",
    )(a, w, res)


def _mlp_kernel(x_ref, g_ref, wu_ref, wd_ref, o_ref, xn_ref):
    f = pl.program_id(1)

    @pl.when(f == 0)
    def _():
        _normalize_rows_into(x_ref, g_ref, xn_ref)

    h = jnp.dot(xn_ref[...], wu_ref[...], preferred_element_type=F32)
    a = jnp.square(jnp.maximum(h, 0.0)).astype(BF16)
    y = jnp.dot(a, wd_ref[...], preferred_element_type=F32)

    @pl.when(f == 0)
    def _():
        o_ref[...] = x_ref[...] + y

    @pl.when(f > 0)
    def _():
        o_ref[...] += y


def mlp_residual(x, g, w_up, w_down, tm=1024, tf=512):
    n, d = x.shape
    dff = w_up.shape[1]
    return pl.pallas_call(
        _mlp_kernel,
        grid=(n // tm, dff // tf),
        in_specs=[pl.BlockSpec((tm, d), lambda i, f: (i, 0)),
                  pl.BlockSpec((1, d), lambda i, f: (0, 0)),
                  pl.BlockSpec((d, tf), lambda i, f: (0, f)),
                  pl.BlockSpec((tf, d), lambda i, f: (f, 0))],
        out_specs=pl.BlockSpec((tm, d), lambda i, f: (i, 0)),
        out_shape=jax.ShapeDtypeStruct((n, d), F32),
        scratch_shapes=[pltpu.VMEM((tm, d), BF16)],
        compiler_params=pltpu.CompilerParams(dimension_semantics=("parallel", "arbitrary")),
        name="mlp_residual",
    )(x, g.reshape(1, d), w_up, w_down)


def _stick_breaking_kernel(q_ref, k_ref, v_ref, o_ref, acc_ref):
    i = pl.program_id(2)
    heads = q_ref.shape[1] // HEAD_DIM
    scale = HEAD_DIM ** -0.5
    row = lax.broadcasted_iota(jnp.int32, (BLK, BLK), 0)
    col = lax.broadcasted_iota(jnp.int32, (BLK, BLK), 1)
    later = (row > col).astype(BF16)
    later2 = jnp.concatenate([later, later], axis=0)
    causal = col < row
    head_cols = [slice(g * HEAD_DIM, (g + 1) * HEAD_DIM) for g in range(heads)]

    def block(j, carries, diagonal):
        start = pl.multiple_of(j * BLK, BLK)
        zs = [lax.dot_general(q_ref[:, cs], k_ref[pl.ds(start, BLK), cs], _NT,
                              preferred_element_type=F32) for cs in head_cols]
        log_bs, log_1mbs, splits = [], [], []
        for z in zs:
            z = z * scale
            log_b = jnp.minimum(z, 0.0) - jnp.log1p(jnp.exp(-jnp.abs(z)))
            log_1mb = log_b - z
            if diagonal:
                log_1mb = jnp.where(causal, log_1mb, 0.0)
            hi = log_1mb.astype(BF16)
            lo = (log_1mb - hi.astype(F32)).astype(BF16)
            log_bs.append(log_b)
            log_1mbs.append(log_1mb)
            splits.append(jnp.concatenate([hi, lo], axis=1))
        tails = jnp.dot(jnp.concatenate(splits, axis=0), later2, preferred_element_type=F32)
        weights = []
        for g in range(heads):
            a = jnp.exp(log_bs[g] + tails[g * BLK:(g + 1) * BLK] + carries[g])
            if diagonal:
                a = jnp.where(causal, a, 0.0)
            weights.append(a.astype(BF16))
        contribs = [jnp.dot(weights[g], v_ref[pl.ds(start, BLK), head_cols[g]],
                            preferred_element_type=F32) for g in range(heads)]
        for g in range(heads):
            if diagonal:
                acc_ref[:, head_cols[g]] = contribs[g]
            else:
                acc_ref[:, head_cols[g]] += contribs[g]
        return tuple(carries[g] + jnp.sum(log_1mbs[g], axis=1, keepdims=True) for g in range(heads))

    carries = block(i, (jnp.zeros((BLK, 1), F32),) * heads, True)

    def cond(state):
        live = functools.reduce(jnp.maximum, state[1:])
        return (state[0] >= 0) & (jnp.max(live) > EXP_IS_ZERO_BELOW)

    def body(state):
        return (state[0] - 1,) + block(state[0], state[1:], False)

    lax.while_loop(cond, body, (i - 1,) + carries)
    o_ref[...] = acc_ref[...].astype(o_ref.dtype)


def stick_breaking_attention(qkv, heads_per_step=8):
    b, s, _ = qkv.shape
    groups = SB_HEADS // heads_per_step
    w = heads_per_step * HEAD_DIM
    return pl.pallas_call(
        _stick_breaking_kernel,
        grid=(b, groups, s // BLK),
        in_specs=[pl.BlockSpec((None, BLK, w), lambda bi, gi, i: (bi, i, gi)),
                  pl.BlockSpec((None, s, w), lambda bi, gi, i: (bi, 0, groups + gi)),
                  pl.BlockSpec((None, s, w), lambda bi, gi, i: (bi, 0, 2 * groups + gi))],
        out_specs=pl.BlockSpec((None, BLK, w), lambda bi, gi, i: (bi, i, gi)),
        out_shape=jax.ShapeDtypeStruct((b, s, SB_HEADS * HEAD_DIM), BF16),
        scratch_shapes=[pltpu.VMEM((BLK, w), F32)],
        compiler_params=pltpu.CompilerParams(
            dimension_semantics=("parallel", "parallel", "arbitrary")),
        name="stick_breaking_attention",
    )(qkv, qkv, qkv)


DSA_Q = DSA_HEADS * HEAD_DIM
DSA_KV = DSA_KV_HEADS * HEAD_DIM
DSA_QI = IDX_HEADS * IDX_HEAD_DIM
DSA_SPLITS = (0, DSA_Q, DSA_Q + DSA_KV, DSA_Q + 2 * DSA_KV, DSA_Q + 2 * DSA_KV + DSA_QI,
              DSA_Q + 2 * DSA_KV + DSA_QI + IDX_HEAD_DIM,
              DSA_Q + 2 * DSA_KV + DSA_QI + IDX_HEAD_DIM + IDX_HEADS)
DSA_IN = DSA_SPLITS[-1]
DSA_IN_PADDED = -(-DSA_IN // 128) * 128
assert DSA_SPLITS[4] % 128 == 0 and DSA_IN_PADDED - DSA_SPLITS[4] == 128 and IDX_HEAD_DIM % 8 == 0


def _dsa_prep_kernel(p_ref, qg_ref, kg_ref, q_ref, k_ref, vt_ref, qi_ref, ki_ref, wit_ref):
    s = DSA_SPLITS
    for h in range(DSA_HEADS):
        lo = s[0] + h * HEAD_DIM
        q_ref[:, h * HEAD_DIM:(h + 1) * HEAD_DIM] = _rms_rows(
            p_ref[:, lo:lo + HEAD_DIM], qg_ref[...]).astype(BF16)
    for h in range(DSA_KV_HEADS):
        lo = s[1] + h * HEAD_DIM
        k_ref[:, h * HEAD_DIM:(h + 1) * HEAD_DIM] = _rms_rows(
            p_ref[:, lo:lo + HEAD_DIM], kg_ref[...]).astype(BF16)
        lo = s[2] + h * HEAD_DIM
        vt_ref[h * HEAD_DIM:(h + 1) * HEAD_DIM, :] = p_ref[:, lo:lo + HEAD_DIM].T.astype(BF16)
    qi_ref[...] = p_ref[:, s[3]:s[4]].astype(BF16)
    ki_ref[...] = p_ref[:, s[4]:s[5]].astype(BF16)
    tail_t = p_ref[:, s[4]:DSA_IN_PADDED].T
    w_t = tail_t[IDX_HEAD_DIM:IDX_HEAD_DIM + IDX_HEADS, :] * (IDX_HEADS ** -0.5 * IDX_HEAD_DIM ** -0.5)
    for e in range(KC // BLK):
        wit_ref[e] = w_t[:, e * BLK:(e + 1) * BLK]


def dsa_prep(proj, q_gain, k_gain):
    n = proj.shape[0]
    nc = n // KC
    row = lambda w: pl.BlockSpec((KC, w), lambda i: (i, 0))
    return pl.pallas_call(
        _dsa_prep_kernel,
        grid=(nc,),
        in_specs=[row(DSA_IN_PADDED),
                  pl.BlockSpec((1, HEAD_DIM), lambda i: (0, 0)),
                  pl.BlockSpec((1, HEAD_DIM), lambda i: (0, 0))],
        out_specs=[row(DSA_Q), row(DSA_KV),
                   pl.BlockSpec((None, DSA_KV, KC), lambda i: (i, 0, 0)),
                   row(DSA_QI), row(IDX_HEAD_DIM),
                   pl.BlockSpec((KC // BLK, IDX_HEADS, BLK), lambda i: (i, 0, 0))],
        out_shape=[jax.ShapeDtypeStruct((n, DSA_Q), BF16),
                   jax.ShapeDtypeStruct((n, DSA_KV), BF16),
                   jax.ShapeDtypeStruct((nc, DSA_KV, KC), BF16),
                   jax.ShapeDtypeStruct((n, DSA_QI), BF16),
                   jax.ShapeDtypeStruct((n, IDX_HEAD_DIM), BF16),
                   jax.ShapeDtypeStruct((n // BLK, IDX_HEADS, BLK), F32)],
        compiler_params=pltpu.CompilerParams(dimension_semantics=("parallel",)),
        name="dsa_prep",
    )(proj, q_gain.reshape(1, HEAD_DIM), k_gain.reshape(1, HEAD_DIM))


def _t5_bucket_tiles():
    half = REL_BUCKETS // 2
    max_exact = half // 2
    s = np.arange(BLK)[:, None]
    t = np.arange(BLK)[None, :]
    tiles = []
    for behind in range(3):
        rel = s - t - behind * BLK
        n = np.abs(rel)
        nf = np.maximum(n, max_exact).astype(np.float64)
        large = max_exact + (np.log(nf / max_exact) / math.log(REL_MAX_DIST / max_exact)
                             * (half - max_exact)).astype(np.int32)
        large = np.minimum(large, half - 1)
        tiles.append(np.where(rel > 0, half, 0) + np.where(n < max_exact, n, large))
    return np.stack(tiles).astype(np.int32)


def _rel_bias_tiles_kernel(rb_ref, bucket_ref, o_ref):
    h = pl.program_id(1)
    bucket = bucket_ref[...]
    acc = jnp.zeros((BLK, BLK), F32)
    for b in range(REL_BUCKETS):
        acc = jnp.where(bucket == b, rb_ref[h, b], acc)
    o_ref[...] = acc


def rel_bias_tiles(rel_bias):
    return pl.pallas_call(
        _rel_bias_tiles_kernel,
        grid=(3, DSA_HEADS),
        in_specs=[pl.BlockSpec(memory_space=pltpu.SMEM),
                  pl.BlockSpec((None, BLK, BLK), lambda d, h: (d, 0, 0))],
        out_specs=pl.BlockSpec((None, BLK, BLK), lambda d, h: (d, 0, h)),
        out_shape=jax.ShapeDtypeStruct((3, BLK, DSA_HEADS * BLK), F32),
        name="rel_bias_tiles",
    )(rel_bias, jnp.asarray(_t5_bucket_tiles()))


def _dsa_attention_kernel(topk, q_ref, k_ref, vt_ref, qi_ref, ki_ref, wit_ref, bias_ref,
                          o_ref, score_ref, sel_ref, acc_ref, qn_ref, qi2_ref):
    i = pl.program_id(1)
    nblk = i + 1
    nch = (nblk + 1) // 2
    key = lax.broadcasted_iota(jnp.int32, (KC, BLK), 0)
    t_local = lax.broadcasted_iota(jnp.int32, (1, BLK), 1)
    n_adm = i * BLK + (t_local // CHUNK + 1) * CHUNK
    n_adm_b = jnp.broadcast_to(n_adm, (KC, BLK))
    keep = jnp.minimum(jnp.minimum(topk, nblk * BLK), n_adm).astype(F32)

    for p in range(IDX_HEADS // 2):
        qi2_ref[p] = jnp.concatenate(
            [qi_ref[:, (2 * p + e) * IDX_HEAD_DIM:(2 * p + e + 1) * IDX_HEAD_DIM] for e in range(2)], axis=0)

    def score_chunk(c, _):
        kic = ki_ref[pl.ds(pl.multiple_of(c * KC, KC), KC), :]
        ds = [lax.dot_general(kic, qi2_ref[p], _NT, preferred_element_type=F32)
              for p in range(IDX_HEADS // 2)]
        s = jnp.zeros((KC, BLK), F32)
        for p in range(IDX_HEADS // 2):
            s = s + jnp.maximum(ds[p][:, :BLK], 0.0) * wit_ref[2 * p:2 * p + 1, :]
            s = s + jnp.maximum(ds[p][:, BLK:], 0.0) * wit_ref[2 * p + 1:2 * p + 2, :]
        score_ref[c] = jnp.where(key + c * KC < n_adm_b, s, -jnp.inf)
        return 0

    lax.fori_loop(0, nch, score_chunk, 0)

    def count(pred, x):
        xb = jnp.broadcast_to(x, (KC, BLK))
        c = lax.fori_loop(0, nch, lambda c, acc: acc + pred(score_ref[c], xb).astype(F32),
                          jnp.zeros((KC, BLK), F32))
        return jnp.sum(c, axis=0, keepdims=True)

    def minmax_body(c, st):
        s = score_ref[c]
        return (jnp.minimum(st[0], jnp.where(s == -jnp.inf, jnp.inf, s)), jnp.maximum(st[1], s))

    mn, mx = lax.fori_loop(0, nch, minmax_body,
                           (jnp.full((KC, BLK), jnp.inf, F32), jnp.full((KC, BLK), -jnp.inf, F32)))
    lo0 = jnp.min(mn, axis=0, keepdims=True)
    hi0 = jnp.max(mx, axis=0, keepdims=True)

    def midpoint(lo, hi):
        return 0.5 * lo + 0.5 * hi

    def bisect_cond(st):
        it, lo, hi, clo = st
        mid = midpoint(lo, hi)
        open_ = (clo > keep) & (mid > lo) & (mid < hi)
        return (it < MAX_BISECT) & (jnp.max(open_.astype(F32)) > 0.0)

    def bisect_body(st):
        it, lo, hi, clo = st
        for _ in range(BISECT_STEPS_PER_CHECK):
            mid = midpoint(lo, hi)
            c = count(lambda s, x: s >= x, mid)
            ge = c >= keep
            lo, hi, clo = jnp.where(ge, mid, lo), jnp.where(ge, hi, mid), jnp.where(ge, c, clo)
        return it + BISECT_STEPS_PER_CHECK, lo, hi, clo

    _, thr, _, cthr = lax.while_loop(bisect_cond, bisect_body,
                                     (jnp.int32(0), lo0, hi0, n_adm.astype(F32)))
    thr_b = jnp.broadcast_to(thr, (KC, BLK))
    tied = jnp.max(cthr - keep) > 0.0

    @pl.when(jnp.logical_not(tied))
    def _():
        def select_chunk(c, _):
            sel_ref[c] = jnp.where(score_ref[c] >= thr_b, 0.0, MASKED)
            return 0

        lax.fori_loop(0, nch, select_chunk, 0)

    @pl.when(tied)
    def _():
        room = jnp.broadcast_to(keep - count(lambda s, x: s > x, thr), (KC, BLK))
        earlier = (lax.broadcasted_iota(jnp.int32, (KC, KC), 1)
                   < lax.broadcasted_iota(jnp.int32, (KC, KC), 0)).astype(BF16)

        def select_chunk(c, seen):
            s = score_ref[c]
            eq = s == thr_b
            rank = jnp.dot(earlier, eq.astype(BF16), preferred_element_type=F32) + seen
            sel_ref[c] = jnp.where((s > thr_b) | (eq & (rank < room)), 0.0, MASKED)
            return seen + jnp.sum(eq.astype(F32), axis=0, keepdims=True)

        lax.fori_loop(0, nch, select_chunk, jnp.zeros((1, BLK), F32))

    cols = DSA_GROUP * BLK
    kv_cols = [slice(n * HEAD_DIM, (n + 1) * HEAD_DIM) for n in range(DSA_KV_HEADS)]
    for n in range(DSA_KV_HEADS):
        qn_ref[n] = jnp.concatenate(
            [q_ref[:, (n * DSA_GROUP + g) * HEAD_DIM:(n * DSA_GROUP + g + 1) * HEAD_DIM]
             for g in range(DSA_GROUP)], axis=0)
    acc_ref[...] = jnp.zeros_like(acc_ref)

    def attend(c, st):
        kstart = pl.multiple_of(c * KC, KC)
        behind = [jnp.clip(i - 2 * c - e, 0, 2) for e in range(KC // BLK)]
        sel = jnp.concatenate([sel_ref[c]] * DSA_GROUP, axis=1)
        raw = [lax.dot_general(k_ref[pl.ds(kstart, KC), kv_cols[n]], qn_ref[n], _NT,
                               preferred_element_type=F32) for n in range(DSA_KV_HEADS)]
        probs, alphas, out = [], [], []
        for n in range(DSA_KV_HEADS):
            m, l = st[2 * n], st[2 * n + 1]
            bias = jnp.concatenate([bias_ref[e, :, n * cols:(n + 1) * cols] for e in behind], axis=0)
            s = raw[n] * HEAD_DIM ** -0.5 + bias + sel
            m_new = jnp.maximum(m, jnp.max(s, axis=0, keepdims=True))
            alpha = jnp.exp(m - m_new)
            p = jnp.exp(s - m_new)
            probs.append(p.astype(BF16))
            alphas.append(alpha)
            out += [m_new, alpha * l + jnp.sum(p, axis=0, keepdims=True)]
        pvs = [jnp.dot(vt_ref[c, kv_cols[n], :], probs[n], preferred_element_type=F32)
               for n in range(DSA_KV_HEADS)]
        for n in range(DSA_KV_HEADS):
            acc_ref[n] = alphas[n] * acc_ref[n] + pvs[n]
        return tuple(out)

    stats = lax.fori_loop(
        0, nch, attend,
        (jnp.full((1, cols), MASKED, F32), jnp.zeros((1, cols), F32)) * DSA_KV_HEADS)
    for n in range(DSA_KV_HEADS):
        o = acc_ref[n] / stats[2 * n + 1]
        for g in range(DSA_GROUP):
            h = n * DSA_GROUP + g
            o_ref[:, h * HEAD_DIM:(h + 1) * HEAD_DIM] = o[:, g * BLK:(g + 1) * BLK].T.astype(o_ref.dtype)


def dsa_attention(q, k, vt, qi, ki, wit, bias_tiles):
    b, s, _ = q.shape
    nb, nc = s // BLK, s // KC
    topk = min(IDX_TOPK_MAX, s // 4)
    return pl.pallas_call(
        functools.partial(_dsa_attention_kernel, topk),
        grid=(b, nb),
        in_specs=[pl.BlockSpec((None, BLK, DSA_Q), lambda bi, i: (bi, i, 0)),
                  pl.BlockSpec((None, s, DSA_KV), lambda bi, i: (bi, 0, 0)),
                  pl.BlockSpec((None, nc, DSA_KV, KC), lambda bi, i: (bi, 0, 0, 0)),
                  pl.BlockSpec((None, BLK, DSA_QI), lambda bi, i: (bi, i, 0)),
                  pl.BlockSpec((None, s, IDX_HEAD_DIM), lambda bi, i: (bi, 0, 0)),
                  pl.BlockSpec((None, None, IDX_HEADS, BLK), lambda bi, i: (bi, i, 0, 0)),
                  pl.BlockSpec((3, BLK, DSA_HEADS * BLK), lambda bi, i: (0, 0, 0))],
        out_specs=pl.BlockSpec((None, BLK, DSA_Q), lambda bi, i: (bi, i, 0)),
        out_shape=jax.ShapeDtypeStruct((b, s, DSA_Q), BF16),
        scratch_shapes=[pltpu.VMEM((nc, KC, BLK), F32), pltpu.VMEM((nc, KC, BLK), F32),
                        pltpu.VMEM((DSA_KV_HEADS, HEAD_DIM, DSA_GROUP * BLK), F32),
                        pltpu.VMEM((DSA_KV_HEADS, DSA_GROUP * BLK, HEAD_DIM), BF16),
                        pltpu.VMEM((IDX_HEADS // 2, 2 * BLK, IDX_HEAD_DIM), BF16)],
        compiler_params=pltpu.CompilerParams(dimension_semantics=("parallel", "arbitrary")),
        name="dsa_attention",
    )(q, k, vt, qi, ki, wit, bias_tiles)


def kernel(x, norm_mix, w_in_a, w_out_a, w_in_b, w_out_b, q_norm_b, k_norm_b, rel_bias,
           norm_mlp, w_up, w_down):
    b, s, d = x.shape
    n = b * s
    nb = s // BLK
    depth = norm_mix.shape[0]
    w_in_b = jnp.pad(w_in_b, ((0, 0), (0, 0), (0, DSA_IN_PADDED - DSA_IN)))
    w_in_a, w_out_a, w_in_b, w_out_b, w_up, w_down = (
        w.astype(BF16) for w in (w_in_a, w_out_a, w_in_b, w_out_b, w_up, w_down))
    bias_tiles = rel_bias_tiles(rel_bias)

    x = x.reshape(n, d)
    for layer in range(depth):
        j = layer // 2
        if layer % 2 == 0:
            qkv = norm_matmul(x, norm_mix[layer], w_in_a[j], BF16)
            o = stick_breaking_attention(qkv.reshape(b, s, -1))
            x = matmul_residual(o.reshape(n, -1), w_out_a[j], x)
        else:
            proj = norm_matmul(x, norm_mix[layer], w_in_b[j], F32, tn=DSA_IN_PADDED // 3)
            q, k, vt, qi, ki, wit = dsa_prep(proj, q_norm_b[j], k_norm_b[j])
            o = dsa_attention(q.reshape(b, s, -1), k.reshape(b, s, -1),
                              vt.reshape(b, s // KC, DSA_KV, KC), qi.reshape(b, s, -1),
                              ki.reshape(b, s, -1), wit.reshape(b, nb, IDX_HEADS, BLK), bias_tiles)
            x = matmul_residual(o.reshape(n, -1), w_out_b[j], x)
        x = mlp_residual(x, norm_mlp[layer], w_up[layer], w_down[layer])
    return x.reshape(b, s, d)
```

```python
import functools
import math

import jax
import jax.numpy as jnp
import numpy as np
from jax import lax
from jax.experimental import pallas as pl
from jax.experimental.pallas import tpu as pltpu

F32 = jnp.float32
BF16 = jnp.bfloat16

RMS_EPS = 1e-6
NORM_STRIP = 128
CHUNK = 64
BLK = 128
KC = 2 * BLK
HEAD_DIM = 128
SB_HEADS = 16
DSA_HEADS = 16
DSA_KV_HEADS = 4
DSA_GROUP = DSA_HEADS // DSA_KV_HEADS
IDX_HEADS = 16
IDX_HEAD_DIM = 64
IDX_TOPK_MAX = 256
REL_BUCKETS = 32
REL_MAX_DIST = 128
MASKED = -1e30
LOG2_E = math.log2(math.e)
MAX_BISECT = 320
BISECT_STEPS_PER_CHECK = 4
EXP_IS_ZERO_BELOW = -104.0

_NT = (((1,), (1,)), ((), ()))


def _rms_rows(x, g):
    return x * lax.rsqrt(jnp.mean(x * x, axis=-1, keepdims=True) + RMS_EPS) * g


def _normalize_rows_into(x_ref, g_ref, xn_ref):
    def strip(r, _):
        rows = pl.ds(pl.multiple_of(r * NORM_STRIP, NORM_STRIP), NORM_STRIP)
        xn_ref[rows, :] = _rms_rows(x_ref[rows, :], g_ref[...]).astype(BF16)
        return 0

    lax.fori_loop(0, x_ref.shape[0] // NORM_STRIP, strip, 0)


def _norm_matmul_kernel(x_ref, g_ref, w_ref, o_ref, xn_ref):
    @pl.when(pl.program_id(1) == 0)
    def _():
        _normalize_rows_into(x_ref, g_ref, xn_ref)

    o_ref[...] = jnp.dot(xn_ref[...], w_ref[...], preferred_element_type=F32).astype(o_ref.dtype)


def norm_matmul(x, g, w, layer, out_dtype, tm=1024, tn=1024):
    n, d = x.shape
    m = w.shape[2]
    return pl.pallas_call(
        _norm_matmul_kernel,
        grid=(n // tm, m // tn),
        in_specs=[pl.BlockSpec((tm, d), lambda i, j: (i, 0)),
                  pl.BlockSpec((1, d), lambda i, j: (0, 0)),
                  pl.BlockSpec((None, d, tn), lambda i, j: (layer, 0, j))],
        out_specs=pl.BlockSpec((tm, tn), lambda i, j: (i, j)),
        out_shape=jax.ShapeDtypeStruct((n, m), out_dtype),
        scratch_shapes=[pltpu.VMEM((tm, d), BF16)],
        compiler_params=pltpu.CompilerParams(dimension_semantics=("parallel", "arbitrary")),
        name="norm_matmul",
    )(x, g.reshape(1, d), w)


def _matmul_residual_kernel(a_ref, w_ref, r_ref, o_ref):
    o_ref[...] = r_ref[...] + jnp.dot(a_ref[...], w_ref[...], preferred_element_type=F32)


def matmul_residual(a, w, layer, res, tm=512):
    n, k = a.shape
    m = w.shape[2]
    return pl.pallas_call(
        _matmul_residual_kernel,
        grid=(n // tm,),
        in_specs=[pl.BlockSpec((tm, k), lambda i: (i, 0)),
                  pl.BlockSpec((None, k, m), lambda i: (layer, 0, 0)),
                  pl.BlockSpec((tm, m), lambda i: (i, 0))],
        out_specs=pl.BlockSpec((tm, m), lambda i: (i, 0)),
        out_shape=jax.ShapeDtypeStruct((n, m), F32),
        compiler_params=pltpu.CompilerParams(dimension_semantics=("parallel",)),
        name="matmul_residual",
    )(a, w, res)


def _mlp_kernel(x_ref, g_ref, wu_ref, wd_ref, o_ref, xn_ref):
    f = pl.program_id(1)

    @pl.when(f == 0)
    def _():
        _normalize_rows_into(x_ref, g_ref, xn_ref)
        o_ref[...] = x_ref[...]

    h = jnp.dot(xn_ref[...], wu_ref[...], preferred_element_type=F32)
    a = jnp.square(jnp.maximum(h, 0.0)).astype(BF16)
    o_ref[...] += jnp.dot(a, wd_ref[...], preferred_element_type=F32)


def mlp_residual(x, g, w_up, w_down, layer, tm=512, tf=1024):
    n, d = x.shape
    dff = w_up.shape[2]
    return pl.pallas_call(
        _mlp_kernel,
        grid=(n // tm, dff // tf),
        in_specs=[pl.BlockSpec((tm, d), lambda i, f: (i, 0)),
                  pl.BlockSpec((1, d), lambda i, f: (0, 0)),
                  pl.BlockSpec((None, d, tf), lambda i, f: (layer, 0, f)),
                  pl.BlockSpec((None, tf, d), lambda i, f: (layer, f, 0))],
        out_specs=pl.BlockSpec((tm, d), lambda i, f: (i, 0)),
        out_shape=jax.ShapeDtypeStruct((n, d), F32),
        scratch_shapes=[pltpu.VMEM((tm, d), BF16)],
        compiler_params=pltpu.CompilerParams(dimension_semantics=("parallel", "arbitrary")),
        name="mlp_residual",
    )(x, g.reshape(1, d), w_up, w_down)


def _stick_breaking_kernel(q_ref, k_ref, v_ref, o_ref, acc_ref):
    i = pl.program_id(2)
    heads = q_ref.shape[1] // HEAD_DIM
    scale = HEAD_DIM ** -0.5
    row = lax.broadcasted_iota(jnp.int32, (BLK, BLK), 0)
    col = lax.broadcasted_iota(jnp.int32, (BLK, BLK), 1)
    later = (row > col).astype(BF16)
    later2 = jnp.concatenate([later, later], axis=0)
    causal = col < row
    head_cols = [slice(g * HEAD_DIM, (g + 1) * HEAD_DIM) for g in range(heads)]

    def block(j, carries, diagonal):
        start = pl.multiple_of(j * BLK, BLK)
        zs = [lax.dot_general(q_ref[:, cs], k_ref[pl.ds(start, BLK), cs], _NT,
                              preferred_element_type=F32) for cs in head_cols]
        log_bs, log_1mbs, splits = [], [], []
        for z in zs:
            z = z * scale
            log_b = jnp.minimum(z, 0.0) - jnp.log(1.0 + jnp.exp(-jnp.abs(z)))
            log_1mb = log_b - z
            if diagonal:
                log_1mb = jnp.where(causal, log_1mb, 0.0)
            hi = log_1mb.astype(BF16)
            lo = (log_1mb - hi.astype(F32)).astype(BF16)
            log_bs.append(log_b)
            log_1mbs.append(log_1mb)
            splits.append(jnp.concatenate([hi, lo], axis=1))
        tails = jnp.dot(jnp.concatenate(splits, axis=0), later2, preferred_element_type=F32)
        weights = []
        for g in range(heads):
            a = jnp.exp(log_bs[g] + tails[g * BLK:(g + 1) * BLK] + carries[g])
            if diagonal:
                a = jnp.where(causal, a, 0.0)
            weights.append(a.astype(BF16))
        contribs = [jnp.dot(weights[g], v_ref[pl.ds(start, BLK), head_cols[g]],
                            preferred_element_type=F32) for g in range(heads)]
        for g in range(heads):
            if diagonal:
                acc_ref[:, head_cols[g]] = contribs[g]
            else:
                acc_ref[:, head_cols[g]] += contribs[g]
        return tuple(carries[g] + jnp.sum(log_1mbs[g], axis=1, keepdims=True) for g in range(heads))

    carries = block(i, (jnp.zeros((BLK, 1), F32),) * heads, True)

    def cond(state):
        live = functools.reduce(jnp.maximum, state[1:])
        return (state[0] >= 0) & (jnp.max(live) > EXP_IS_ZERO_BELOW)

    def body(state):
        return (state[0] - 1,) + block(state[0], state[1:], False)

    lax.while_loop(cond, body, (i - 1,) + carries)
    o_ref[...] = acc_ref[...].astype(o_ref.dtype)


def stick_breaking_attention(qkv, heads_per_step=8):
    b, s, _ = qkv.shape
    groups = SB_HEADS // heads_per_step
    w = heads_per_step * HEAD_DIM
    return pl.pallas_call(
        _stick_breaking_kernel,
        grid=(b, groups, s // BLK),
        in_specs=[pl.BlockSpec((None, BLK, w), lambda bi, gi, i: (bi, i, gi)),
                  pl.BlockSpec((None, s, w), lambda bi, gi, i: (bi, 0, groups + gi)),
                  pl.BlockSpec((None, s, w), lambda bi, gi, i: (bi, 0, 2 * groups + gi))],
        out_specs=pl.BlockSpec((None, BLK, w), lambda bi, gi, i: (bi, i, gi)),
        out_shape=jax.ShapeDtypeStruct((b, s, SB_HEADS * HEAD_DIM), BF16),
        scratch_shapes=[pltpu.VMEM((BLK, w), F32)],
        compiler_params=pltpu.CompilerParams(
            dimension_semantics=("parallel", "parallel", "arbitrary")),
        name="stick_breaking_attention",
    )(qkv, qkv, qkv)


DSA_Q = DSA_HEADS * HEAD_DIM
DSA_KV = DSA_KV_HEADS * HEAD_DIM
DSA_QI = IDX_HEADS * IDX_HEAD_DIM
DSA_SPLITS = (0, DSA_Q, DSA_Q + DSA_KV, DSA_Q + 2 * DSA_KV, DSA_Q + 2 * DSA_KV + DSA_QI,
              DSA_Q + 2 * DSA_KV + DSA_QI + IDX_HEAD_DIM,
              DSA_Q + 2 * DSA_KV + DSA_QI + IDX_HEAD_DIM + IDX_HEADS)
DSA_IN = DSA_SPLITS[-1]
DSA_IN_PADDED = -(-DSA_IN // 128) * 128
assert DSA_SPLITS[4] % 128 == 0 and DSA_IN_PADDED - DSA_SPLITS[4] == 128 and IDX_HEAD_DIM % 8 == 0


def _dsa_prep_kernel(p_ref, qg_ref, kg_ref, q_ref, k_ref, vt_ref, qi_ref, ki_ref, wit_ref):
    s = DSA_SPLITS
    for h in range(DSA_HEADS):
        lo = s[0] + h * HEAD_DIM
        q_ref[:, h * HEAD_DIM:(h + 1) * HEAD_DIM] = _rms_rows(
            p_ref[:, lo:lo + HEAD_DIM], qg_ref[...]).astype(BF16)
    for h in range(DSA_KV_HEADS):
        lo = s[1] + h * HEAD_DIM
        k_ref[:, h * HEAD_DIM:(h + 1) * HEAD_DIM] = _rms_rows(
            p_ref[:, lo:lo + HEAD_DIM], kg_ref[...]).astype(BF16)
        lo = s[2] + h * HEAD_DIM
        vt_ref[h * HEAD_DIM:(h + 1) * HEAD_DIM, :] = p_ref[:, lo:lo + HEAD_DIM].T.astype(BF16)
    qi_ref[...] = p_ref[:, s[3]:s[4]].astype(BF16)
    ki_ref[...] = p_ref[:, s[4]:s[5]].astype(BF16)
    tail_t = p_ref[:, s[4]:DSA_IN_PADDED].T
    w_t = tail_t[IDX_HEAD_DIM:IDX_HEAD_DIM + IDX_HEADS, :] * (IDX_HEADS ** -0.5 * IDX_HEAD_DIM ** -0.5)
    for e in range(KC // BLK):
        wit_ref[e] = w_t[:, e * BLK:(e + 1) * BLK]


def dsa_prep(proj, q_gain, k_gain):
    n = proj.shape[0]
    nc = n // KC
    row = lambda w: pl.BlockSpec((KC, w), lambda i: (i, 0))
    return pl.pallas_call(
        _dsa_prep_kernel,
        grid=(nc,),
        in_specs=[row(DSA_IN_PADDED),
                  pl.BlockSpec((1, HEAD_DIM), lambda i: (0, 0)),
                  pl.BlockSpec((1, HEAD_DIM), lambda i: (0, 0))],
        out_specs=[row(DSA_Q), row(DSA_KV),
                   pl.BlockSpec((None, DSA_KV, KC), lambda i: (i, 0, 0)),
                   row(DSA_QI), row(IDX_HEAD_DIM),
                   pl.BlockSpec((KC // BLK, IDX_HEADS, BLK), lambda i: (i, 0, 0))],
        out_shape=[jax.ShapeDtypeStruct((n, DSA_Q), BF16),
                   jax.ShapeDtypeStruct((n, DSA_KV), BF16),
                   jax.ShapeDtypeStruct((nc, DSA_KV, KC), BF16),
                   jax.ShapeDtypeStruct((n, DSA_QI), BF16),
                   jax.ShapeDtypeStruct((n, IDX_HEAD_DIM), BF16),
                   jax.ShapeDtypeStruct((n // BLK, IDX_HEADS, BLK), F32)],
        compiler_params=pltpu.CompilerParams(dimension_semantics=("parallel",)),
        name="dsa_prep",
    )(proj, q_gain.reshape(1, HEAD_DIM), k_gain.reshape(1, HEAD_DIM))


def _t5_bucket_tiles():
    half = REL_BUCKETS // 2
    max_exact = half // 2
    s = np.arange(BLK)[:, None]
    t = np.arange(BLK)[None, :]
    tiles = []
    for behind in range(3):
        rel = s - t - behind * BLK
        n = np.abs(rel)
        nf = np.maximum(n, max_exact).astype(np.float64)
        large = max_exact + (np.log(nf / max_exact) / math.log(REL_MAX_DIST / max_exact)
                             * (half - max_exact)).astype(np.int32)
        large = np.minimum(large, half - 1)
        tiles.append(np.where(rel > 0, half, 0) + np.where(n < max_exact, n, large))
    return np.stack(tiles).astype(np.int32)


def _rel_bias_tiles_kernel(rb_ref, bucket_ref, o_ref):
    h = pl.program_id(1)
    bucket = bucket_ref[...]
    acc = jnp.zeros((BLK, BLK), F32)
    for b in range(REL_BUCKETS):
        acc = jnp.where(bucket == b, rb_ref[h, b], acc)
    o_ref[...] = acc * LOG2_E


def rel_bias_tiles(rel_bias):
    return pl.pallas_call(
        _rel_bias_tiles_kernel,
        grid=(3, DSA_HEADS),
        in_specs=[pl.BlockSpec(memory_space=pltpu.SMEM),
                  pl.BlockSpec((None, BLK, BLK), lambda d, h: (d, 0, 0))],
        out_specs=pl.BlockSpec((None, BLK, BLK), lambda d, h: (d, 0, h)),
        out_shape=jax.ShapeDtypeStruct((3, BLK, DSA_HEADS * BLK), F32),
        name="rel_bias_tiles",
    )(rel_bias, jnp.asarray(_t5_bucket_tiles()))


def _dsa_attention_kernel(topk, q_ref, k_ref, vt_ref, qi_ref, ki_ref, wit_ref, bias_ref,
                          o_ref, score_ref, sel_ref, acc_ref, qn_ref, qi2_ref):
    i = pl.program_id(1)
    nblk = i + 1
    nch = (nblk + 1) // 2
    key = lax.broadcasted_iota(jnp.int32, (KC, BLK), 0)
    t_local = lax.broadcasted_iota(jnp.int32, (1, BLK), 1)
    n_adm = i * BLK + (t_local // CHUNK + 1) * CHUNK
    n_adm_b = jnp.broadcast_to(n_adm, (KC, BLK))
    keep = jnp.minimum(jnp.minimum(topk, nblk * BLK), n_adm).astype(F32)

    for p in range(IDX_HEADS // 2):
        qi2_ref[p] = jnp.concatenate(
            [qi_ref[:, (2 * p + e) * IDX_HEAD_DIM:(2 * p + e + 1) * IDX_HEAD_DIM] for e in range(2)], axis=0)

    def score_chunk(c, _):
        kic = ki_ref[pl.ds(pl.multiple_of(c * KC, KC), KC), :]
        ds = [lax.dot_general(kic, qi2_ref[p], _NT, preferred_element_type=F32)
              for p in range(IDX_HEADS // 2)]
        s = jnp.zeros((KC, BLK), F32)
        for p in range(IDX_HEADS // 2):
            s = s + jnp.maximum(ds[p][:, :BLK], 0.0) * wit_ref[2 * p:2 * p + 1, :]
            s = s + jnp.maximum(ds[p][:, BLK:], 0.0) * wit_ref[2 * p + 1:2 * p + 2, :]
        score_ref[c] = jnp.where(key + c * KC < n_adm_b, s, -jnp.inf)
        return 0

    lax.fori_loop(0, nch, score_chunk, 0)

    def count(pred, x):
        xb = jnp.broadcast_to(x, (KC, BLK))
        c = lax.fori_loop(0, nch, lambda c, acc: acc + pred(score_ref[c], xb).astype(F32),
                          jnp.zeros((KC, BLK), F32))
        return jnp.sum(c, axis=0, keepdims=True)

    def minmax_body(c, st):
        s = score_ref[c]
        return (jnp.minimum(st[0], jnp.where(s == -jnp.inf, jnp.inf, s)), jnp.maximum(st[1], s))

    mn, mx = lax.fori_loop(0, nch, minmax_body,
                           (jnp.full((KC, BLK), jnp.inf, F32), jnp.full((KC, BLK), -jnp.inf, F32)))
    lo0 = jnp.min(mn, axis=0, keepdims=True)
    hi0 = jnp.max(mx, axis=0, keepdims=True)

    def midpoint(lo, hi):
        return 0.5 * lo + 0.5 * hi

    def bisect_cond(st):
        it, lo, hi, clo = st
        mid = midpoint(lo, hi)
        open_ = (clo > keep) & (mid > lo) & (mid < hi)
        return (it < MAX_BISECT) & (jnp.max(open_.astype(F32)) > 0.0)

    def bisect_body(st):
        it, lo, hi, clo = st
        for _ in range(BISECT_STEPS_PER_CHECK):
            mid = midpoint(lo, hi)
            c = count(lambda s, x: s >= x, mid)
            ge = c >= keep
            lo, hi, clo = jnp.where(ge, mid, lo), jnp.where(ge, hi, mid), jnp.where(ge, c, clo)
        return it + BISECT_STEPS_PER_CHECK, lo, hi, clo

    _, thr, _, cthr = lax.while_loop(bisect_cond, bisect_body,
                                     (jnp.int32(0), lo0, hi0, n_adm.astype(F32)))
    thr_b = jnp.broadcast_to(thr, (KC, BLK))
    tied = jnp.max(cthr - keep) > 0.0

    @pl.when(jnp.logical_not(tied))
    def _():
        def select_chunk(c, _):
            sel_ref[c] = jnp.where(score_ref[c] >= thr_b, 0.0, MASKED)
            return 0

        lax.fori_loop(0, nch, select_chunk, 0)

    @pl.when(tied)
    def _():
        room = jnp.broadcast_to(keep - count(lambda s, x: s > x, thr), (KC, BLK))
        earlier = (lax.broadcasted_iota(jnp.int32, (KC, KC), 1)
                   < lax.broadcasted_iota(jnp.int32, (KC, KC), 0)).astype(BF16)

        def select_chunk(c, seen):
            s = score_ref[c]
            eq = s == thr_b
            rank = jnp.dot(earlier, eq.astype(BF16), preferred_element_type=F32) + seen
            sel_ref[c] = jnp.where((s > thr_b) | (eq & (rank < room)), 0.0, MASKED)
            return seen + jnp.sum(eq.astype(F32), axis=0, keepdims=True)

        lax.fori_loop(0, nch, select_chunk, jnp.zeros((1, BLK), F32))

    cols = DSA_GROUP * BLK
    kv_cols = [slice(n * HEAD_DIM, (n + 1) * HEAD_DIM) for n in range(DSA_KV_HEADS)]
    for n in range(DSA_KV_HEADS):
        qn_ref[n] = jnp.concatenate(
            [q_ref[:, (n * DSA_GROUP + g) * HEAD_DIM:(n * DSA_GROUP + g + 1) * HEAD_DIM]
             for g in range(DSA_GROUP)], axis=0)
    acc_ref[...] = jnp.zeros_like(acc_ref)

    def attend(c, st):
        kstart = pl.multiple_of(c * KC, KC)
        behind = [jnp.clip(i - 2 * c - e, 0, 2) for e in range(KC // BLK)]
        sel = jnp.concatenate([sel_ref[c]] * DSA_GROUP, axis=1)
        raw = [lax.dot_general(k_ref[pl.ds(kstart, KC), kv_cols[n]], qn_ref[n], _NT,
                               preferred_element_type=F32) for n in range(DSA_KV_HEADS)]
        probs, alphas, out = [], [], []
        for n in range(DSA_KV_HEADS):
            m, l = st[2 * n], st[2 * n + 1]
            bias = jnp.concatenate([bias_ref[e, :, n * cols:(n + 1) * cols] for e in behind], axis=0)
            s = raw[n] * (HEAD_DIM ** -0.5 * LOG2_E) + bias + sel
            m_new = jnp.maximum(m, jnp.max(s, axis=0, keepdims=True))
            alpha = jnp.exp2(m - m_new)
            p = jnp.exp2(s - m_new)
            probs.append(p.astype(BF16))
            alphas.append(alpha)
            out += [m_new, alpha * l + jnp.sum(p, axis=0, keepdims=True)]
        pvs = [jnp.dot(vt_ref[c, kv_cols[n], :], probs[n], preferred_element_type=F32)
               for n in range(DSA_KV_HEADS)]
        for n in range(DSA_KV_HEADS):
            acc_ref[n] = alphas[n] * acc_ref[n] + pvs[n]
        return tuple(out)

    stats = lax.fori_loop(
        0, nch, attend,
        (jnp.full((1, cols), MASKED, F32), jnp.zeros((1, cols), F32)) * DSA_KV_HEADS)
    for n in range(DSA_KV_HEADS):
        o = acc_ref[n] / stats[2 * n + 1]
        for g in range(DSA_GROUP):
            h = n * DSA_GROUP + g
            o_ref[:, h * HEAD_DIM:(h + 1) * HEAD_DIM] = o[:, g * BLK:(g + 1) * BLK].T.astype(o_ref.dtype)


def dsa_attention(q, k, vt, qi, ki, wit, bias_tiles):
    b, s, _ = q.shape
    nb, nc = s // BLK, s // KC
    topk = min(IDX_TOPK_MAX, s // 4)
    return pl.pallas_call(
        functools.partial(_dsa_attention_kernel, topk),
        grid=(b, nb),
        in_specs=[pl.BlockSpec((None, BLK, DSA_Q), lambda bi, i: (bi, i, 0)),
                  pl.BlockSpec((None, s, DSA_KV), lambda bi, i: (bi, 0, 0)),
                  pl.BlockSpec((None, nc, DSA_KV, KC), lambda bi, i: (bi, 0, 0, 0)),
                  pl.BlockSpec((None, BLK, DSA_QI), lambda bi, i: (bi, i, 0)),
                  pl.BlockSpec((None, s, IDX_HEAD_DIM), lambda bi, i: (bi, 0, 0)),
                  pl.BlockSpec((None, None, IDX_HEADS, BLK), lambda bi, i: (bi, i, 0, 0)),
                  pl.BlockSpec((3, BLK, DSA_HEADS * BLK), lambda bi, i: (0, 0, 0))],
        out_specs=pl.BlockSpec((None, BLK, DSA_Q), lambda bi, i: (bi, i, 0)),
        out_shape=jax.ShapeDtypeStruct((b, s, DSA_Q), BF16),
        scratch_shapes=[pltpu.VMEM((nc, KC, BLK), F32), pltpu.VMEM((nc, KC, BLK), F32),
                        pltpu.VMEM((DSA_KV_HEADS, HEAD_DIM, DSA_GROUP * BLK), F32),
                        pltpu.VMEM((DSA_KV_HEADS, DSA_GROUP * BLK, HEAD_DIM), BF16),
                        pltpu.VMEM((IDX_HEADS // 2, 2 * BLK, IDX_HEAD_DIM), BF16)],
        compiler_params=pltpu.CompilerParams(dimension_semantics=("parallel", "arbitrary")),
        name="dsa_attention",
    )(q, k, vt, qi, ki, wit, bias_tiles)


def kernel(x, norm_mix, w_in_a, w_out_a, w_in_b, w_out_b, q_norm_b, k_norm_b, rel_bias,
           norm_mlp, w_up, w_down):
    b, s, d = x.shape
    n = b * s
    nb = s // BLK
    depth = norm_mix.shape[0]
    w_in_b = jnp.pad(w_in_b, ((0, 0), (0, 0), (0, DSA_IN_PADDED - DSA_IN)))
    w_in_a, w_out_a, w_in_b, w_out_b, w_up, w_down = (
        w.astype(BF16) for w in (w_in_a, w_out_a, w_in_b, w_out_b, w_up, w_down))
    bias_tiles = rel_bias_tiles(rel_bias)

    x = x.reshape(n, d)
    for layer in range(depth):
        j = layer // 2
        if layer % 2 == 0:
            qkv = norm_matmul(x, norm_mix[layer], w_in_a, j, BF16)
            o = stick_breaking_attention(qkv.reshape(b, s, -1))
            x = matmul_residual(o.reshape(n, -1), w_out_a, j, x)
        else:
            proj = norm_matmul(x, norm_mix[layer], w_in_b, j, F32, tn=DSA_IN_PADDED // 3)
            q, k, vt, qi, ki, wit = dsa_prep(proj, q_norm_b[j], k_norm_b[j])
            o = dsa_attention(q.reshape(b, s, -1), k.reshape(b, s, -1),
                              vt.reshape(b, s // KC, DSA_KV, KC), qi.reshape(b, s, -1),
                              ki.reshape(b, s, -1), wit.reshape(b, nb, IDX_HEADS, BLK), bias_tiles)
            x = matmul_residual(o.reshape(n, -1), w_out_b, j, x)
        x = mlp_residual(x, norm_mlp[layer], w_up, w_down, layer)
    return x.reshape(b, s, d)
```

```python
import functools
import math

import jax
import jax.numpy as jnp
import numpy as np
from jax import lax
from jax.experimental import pallas as pl
from jax.experimental.pallas import tpu as pltpu

F32 = jnp.float32
BF16 = jnp.bfloat16

RMS_EPS = 1e-6
NORM_STRIP = 128
CHUNK = 64
BLK = 128
KC = 2 * BLK
HEAD_DIM = 128
SB_HEADS = 16
DSA_HEADS = 16
DSA_KV_HEADS = 4
DSA_GROUP = DSA_HEADS // DSA_KV_HEADS
IDX_HEADS = 16
IDX_HEAD_DIM = 64
IDX_TOPK_MAX = 256
REL_BUCKETS = 32
REL_MAX_DIST = 128
MASKED = -1e30
LOG2_E = math.log2(math.e)
MAX_BISECT = 320
BISECT_STEPS_PER_CHECK = 4
EXP_IS_ZERO_BELOW = -104.0

_NT = (((1,), (1,)), ((), ()))


def _rms_rows(x, g):
    return x * lax.rsqrt(jnp.mean(x * x, axis=-1, keepdims=True) + RMS_EPS) * g


def _normalize_rows_into(x_ref, g_ref, xn_ref):
    def strip(r, _):
        rows = pl.ds(pl.multiple_of(r * NORM_STRIP, NORM_STRIP), NORM_STRIP)
        xn_ref[rows, :] = _rms_rows(x_ref[rows, :], g_ref[...]).astype(BF16)
        return 0

    lax.fori_loop(0, x_ref.shape[0] // NORM_STRIP, strip, 0)


def _norm_matmul_kernel(x_ref, g_ref, w_ref, o_ref, xn_ref):
    @pl.when(pl.program_id(1) == 0)
    def _():
        _normalize_rows_into(x_ref, g_ref, xn_ref)

    o_ref[...] = jnp.dot(xn_ref[...], w_ref[...], preferred_element_type=F32).astype(o_ref.dtype)


def norm_matmul(x, g, w, layer, out_dtype, tm=1024, tn=1024):
    n, d = x.shape
    m = w.shape[2]
    return pl.pallas_call(
        _norm_matmul_kernel,
        grid=(n // tm, m // tn),
        in_specs=[pl.BlockSpec((tm, d), lambda i, j: (i, 0)),
                  pl.BlockSpec((1, d), lambda i, j: (0, 0)),
                  pl.BlockSpec((None, d, tn), lambda i, j: (layer, 0, j))],
        out_specs=pl.BlockSpec((tm, tn), lambda i, j: (i, j)),
        out_shape=jax.ShapeDtypeStruct((n, m), out_dtype),
        scratch_shapes=[pltpu.VMEM((tm, d), BF16)],
        compiler_params=pltpu.CompilerParams(dimension_semantics=("parallel", "arbitrary")),
        name="norm_matmul",
    )(x, g.reshape(1, d), w)


def _matmul_residual_kernel(a_ref, w_ref, r_ref, o_ref):
    o_ref[...] = r_ref[...] + jnp.dot(a_ref[...], w_ref[...], preferred_element_type=F32)


def matmul_residual(a, w, layer, res, tm=512):
    n, k = a.shape
    m = w.shape[2]
    return pl.pallas_call(
        _matmul_residual_kernel,
        grid=(n // tm,),
        in_specs=[pl.BlockSpec((tm, k), lambda i: (i, 0)),
                  pl.BlockSpec((None, k, m), lambda i: (layer, 0, 0)),
                  pl.BlockSpec((tm, m), lambda i: (i, 0))],
        out_specs=pl.BlockSpec((tm, m), lambda i: (i, 0)),
        out_shape=jax.ShapeDtypeStruct((n, m), F32),
        compiler_params=pltpu.CompilerParams(dimension_semantics=("parallel",)),
        name="matmul_residual",
    )(a, w, res)


def _mlp_kernel(x_ref, g_ref, wu_ref, wd_ref, o_ref, xn_ref):
    f = pl.program_id(1)

    @pl.when(f == 0)
    def _():
        _normalize_rows_into(x_ref, g_ref, xn_ref)
        o_ref[...] = x_ref[...]

    h = jnp.dot(xn_ref[...], wu_ref[...], preferred_element_type=F32)
    a = jnp.square(jnp.maximum(h, 0.0)).astype(BF16)
    o_ref[...] += jnp.dot(a, wd_ref[...], preferred_element_type=F32)


def mlp_residual(x, g, w_up, w_down, layer, tm=512, tf=1024):
    n, d = x.shape
    dff = w_up.shape[2]
    return pl.pallas_call(
        _mlp_kernel,
        grid=(n // tm, dff // tf),
        in_specs=[pl.BlockSpec((tm, d), lambda i, f: (i, 0)),
                  pl.BlockSpec((1, d), lambda i, f: (0, 0)),
                  pl.BlockSpec((None, d, tf), lambda i, f: (layer, 0, f)),
                  pl.BlockSpec((None, tf, d), lambda i, f: (layer, f, 0))],
        out_specs=pl.BlockSpec((tm, d), lambda i, f: (i, 0)),
        out_shape=jax.ShapeDtypeStruct((n, d), F32),
        scratch_shapes=[pltpu.VMEM((tm, d), BF16)],
        compiler_params=pltpu.CompilerParams(dimension_semantics=("parallel", "arbitrary")),
        name="mlp_residual",
    )(x, g.reshape(1, d), w_up, w_down)


def _stick_breaking_kernel(q_ref, k_ref, v_ref, o_ref, acc_ref):
    i = pl.program_id(2)
    heads = q_ref.shape[1] // HEAD_DIM
    scale = HEAD_DIM ** -0.5
    row = lax.broadcasted_iota(jnp.int32, (BLK, BLK), 0)
    col = lax.broadcasted_iota(jnp.int32, (BLK, BLK), 1)
    later = (row > col).astype(BF16)
    later2 = jnp.concatenate([later, later], axis=0)
    causal = col < row
    head_cols = [slice(g * HEAD_DIM, (g + 1) * HEAD_DIM) for g in range(heads)]

    def block(j, carries, diagonal):
        start = pl.multiple_of(j * BLK, BLK)
        zs = [lax.dot_general(q_ref[:, cs], k_ref[pl.ds(start, BLK), cs], _NT,
                              preferred_element_type=F32) for cs in head_cols]
        log_bs, log_1mbs, splits = [], [], []
        for z in zs:
            z = z * scale
            log_b = jnp.minimum(z, 0.0) - jnp.log(1.0 + jnp.exp(-jnp.abs(z)))
            log_1mb = log_b - z
            if diagonal:
                log_1mb = jnp.where(causal, log_1mb, 0.0)
            hi = log_1mb.astype(BF16)
            lo = (log_1mb - hi.astype(F32)).astype(BF16)
            log_bs.append(log_b)
            log_1mbs.append(log_1mb)
            splits.append(jnp.concatenate([hi, lo], axis=1))
        tails = jnp.dot(jnp.concatenate(splits, axis=0), later2, preferred_element_type=F32)
        weights = []
        for g in range(heads):
            a = jnp.exp(log_bs[g] + tails[g * BLK:(g + 1) * BLK] + carries[g])
            if diagonal:
                a = jnp.where(causal, a, 0.0)
            weights.append(a.astype(BF16))
        contribs = [jnp.dot(weights[g], v_ref[pl.ds(start, BLK), head_cols[g]],
                            preferred_element_type=F32) for g in range(heads)]
        for g in range(heads):
            if diagonal:
                acc_ref[:, head_cols[g]] = contribs[g]
            else:
                acc_ref[:, head_cols[g]] += contribs[g]
        return tuple(carries[g] + jnp.sum(log_1mbs[g], axis=1, keepdims=True) for g in range(heads))

    carries = block(i, (jnp.zeros((BLK, 1), F32),) * heads, True)

    def cond(state):
        live = functools.reduce(jnp.maximum, state[1:])
        return (state[0] >= 0) & (jnp.max(live) > EXP_IS_ZERO_BELOW)

    def body(state):
        return (state[0] - 1,) + block(state[0], state[1:], False)

    lax.while_loop(cond, body, (i - 1,) + carries)
    o_ref[...] = acc_ref[...].astype(o_ref.dtype)


def stick_breaking_attention(qkv, heads_per_step=8):
    b, s, _ = qkv.shape
    groups = SB_HEADS // heads_per_step
    w = heads_per_step * HEAD_DIM
    return pl.pallas_call(
        _stick_breaking_kernel,
        grid=(b, groups, s // BLK),
        in_specs=[pl.BlockSpec((None, BLK, w), lambda bi, gi, i: (bi, i, gi)),
                  pl.BlockSpec((None, s, w), lambda bi, gi, i: (bi, 0, groups + gi)),
                  pl.BlockSpec((None, s, w), lambda bi, gi, i: (bi, 0, 2 * groups + gi))],
        out_specs=pl.BlockSpec((None, BLK, w), lambda bi, gi, i: (bi, i, gi)),
        out_shape=jax.ShapeDtypeStruct((b, s, SB_HEADS * HEAD_DIM), BF16),
        scratch_shapes=[pltpu.VMEM((BLK, w), F32)],
        compiler_params=pltpu.CompilerParams(
            dimension_semantics=("parallel", "parallel", "arbitrary")),
        name="stick_breaking_attention",
    )(qkv, qkv, qkv)


DSA_Q = DSA_HEADS * HEAD_DIM
DSA_KV = DSA_KV_HEADS * HEAD_DIM
DSA_QI = IDX_HEADS * IDX_HEAD_DIM
DSA_SPLITS = (0, DSA_Q, DSA_Q + DSA_KV, DSA_Q + 2 * DSA_KV, DSA_Q + 2 * DSA_KV + DSA_QI,
              DSA_Q + 2 * DSA_KV + DSA_QI + IDX_HEAD_DIM,
              DSA_Q + 2 * DSA_KV + DSA_QI + IDX_HEAD_DIM + IDX_HEADS)
DSA_IN = DSA_SPLITS[-1]
DSA_IN_PADDED = -(-DSA_IN // 128) * 128
assert DSA_SPLITS[4] % 128 == 0 and DSA_IN_PADDED - DSA_SPLITS[4] == 128 and IDX_HEAD_DIM % 8 == 0


def _dsa_prep_kernel(p_ref, qg_ref, kg_ref, q_ref, k_ref, vt_ref, qi_ref, ki_ref, wit_ref):
    s = DSA_SPLITS
    for h in range(DSA_HEADS):
        lo = s[0] + h * HEAD_DIM
        q_ref[:, h * HEAD_DIM:(h + 1) * HEAD_DIM] = _rms_rows(
            p_ref[:, lo:lo + HEAD_DIM], qg_ref[...]).astype(BF16)
    for h in range(DSA_KV_HEADS):
        lo = s[1] + h * HEAD_DIM
        k_ref[:, h * HEAD_DIM:(h + 1) * HEAD_DIM] = _rms_rows(
            p_ref[:, lo:lo + HEAD_DIM], kg_ref[...]).astype(BF16)
        lo = s[2] + h * HEAD_DIM
        vt_ref[h * HEAD_DIM:(h + 1) * HEAD_DIM, :] = p_ref[:, lo:lo + HEAD_DIM].T.astype(BF16)
    qi_ref[...] = p_ref[:, s[3]:s[4]].astype(BF16)
    ki_ref[...] = p_ref[:, s[4]:s[5]].astype(BF16)
    tail_t = p_ref[:, s[4]:DSA_IN_PADDED].T
    w_t = tail_t[IDX_HEAD_DIM:IDX_HEAD_DIM + IDX_HEADS, :] * (IDX_HEADS ** -0.5 * IDX_HEAD_DIM ** -0.5)
    for e in range(KC // BLK):
        wit_ref[e] = w_t[:, e * BLK:(e + 1) * BLK]


def dsa_prep(proj, q_gain, k_gain):
    n = proj.shape[0]
    nc = n // KC
    row = lambda w: pl.BlockSpec((KC, w), lambda i: (i, 0))
    return pl.pallas_call(
        _dsa_prep_kernel,
        grid=(nc,),
        in_specs=[row(DSA_IN_PADDED),
                  pl.BlockSpec((1, HEAD_DIM), lambda i: (0, 0)),
                  pl.BlockSpec((1, HEAD_DIM), lambda i: (0, 0))],
        out_specs=[row(DSA_Q), row(DSA_KV),
                   pl.BlockSpec((None, DSA_KV, KC), lambda i: (i, 0, 0)),
                   row(DSA_QI), row(IDX_HEAD_DIM),
                   pl.BlockSpec((KC // BLK, IDX_HEADS, BLK), lambda i: (i, 0, 0))],
        out_shape=[jax.ShapeDtypeStruct((n, DSA_Q), BF16),
                   jax.ShapeDtypeStruct((n, DSA_KV), BF16),
                   jax.ShapeDtypeStruct((nc, DSA_KV, KC), BF16),
                   jax.ShapeDtypeStruct((n, DSA_QI), BF16),
                   jax.ShapeDtypeStruct((n, IDX_HEAD_DIM), BF16),
                   jax.ShapeDtypeStruct((n // BLK, IDX_HEADS, BLK), F32)],
        compiler_params=pltpu.CompilerParams(dimension_semantics=("parallel",)),
        name="dsa_prep",
    )(proj, q_gain.reshape(1, HEAD_DIM), k_gain.reshape(1, HEAD_DIM))


def _t5_bucket_tiles():
    half = REL_BUCKETS // 2
    max_exact = half // 2
    s = np.arange(BLK)[:, None]
    t = np.arange(BLK)[None, :]
    tiles = []
    for behind in range(3):
        rel = s - t - behind * BLK
        n = np.abs(rel)
        nf = np.maximum(n, max_exact).astype(np.float64)
        large = max_exact + (np.log(nf / max_exact) / math.log(REL_MAX_DIST / max_exact)
                             * (half - max_exact)).astype(np.int32)
        large = np.minimum(large, half - 1)
        tiles.append(np.where(rel > 0, half, 0) + np.where(n < max_exact, n, large))
    return np.stack(tiles).astype(np.int32)


def _rel_bias_tiles_kernel(rb_ref, bucket_ref, o_ref):
    h = pl.program_id(1)
    bucket = bucket_ref[...]
    acc = jnp.zeros((BLK, BLK), F32)
    for b in range(REL_BUCKETS):
        acc = jnp.where(bucket == b, rb_ref[h, b], acc)
    o_ref[...] = acc * LOG2_E


def rel_bias_tiles(rel_bias):
    return pl.pallas_call(
        _rel_bias_tiles_kernel,
        grid=(3, DSA_HEADS),
        in_specs=[pl.BlockSpec(memory_space=pltpu.SMEM),
                  pl.BlockSpec((None, BLK, BLK), lambda d, h: (d, 0, 0))],
        out_specs=pl.BlockSpec((None, BLK, BLK), lambda d, h: (d, 0, h)),
        out_shape=jax.ShapeDtypeStruct((3, BLK, DSA_HEADS * BLK), F32),
        name="rel_bias_tiles",
    )(rel_bias, jnp.asarray(_t5_bucket_tiles()))


def _dsa_attention_kernel(topk, q_ref, k_ref, vt_ref, qi_ref, ki_ref, wit_ref, bias_ref,
                          o_ref, score_ref, sel_ref, acc_ref, qn_ref, qi2_ref):
    tile = pl.program_id(1)
    nch = tile + 1
    qblocks = KC // BLK
    key = lax.broadcasted_iota(jnp.int32, (KC, KC), 0)
    t_local = lax.broadcasted_iota(jnp.int32, (1, KC), 1)
    n_adm = tile * KC + (t_local // CHUNK + 1) * CHUNK
    n_adm_b = jnp.broadcast_to(n_adm, (KC, KC))
    block_end = tile * KC + (t_local // BLK + 1) * BLK
    keep = jnp.minimum(jnp.minimum(topk, block_end), n_adm).astype(F32)

    for p in range(IDX_HEADS // 2):
        qi2_ref[p] = jnp.concatenate(
            [qi_ref[:, (2 * p + e) * IDX_HEAD_DIM:(2 * p + e + 1) * IDX_HEAD_DIM] for e in range(2)], axis=0)
    w_rows = jnp.concatenate([wit_ref[e] for e in range(qblocks)], axis=1)

    def score_chunk(c, _):
        kic = ki_ref[pl.ds(pl.multiple_of(c * KC, KC), KC), :]
        ds = [lax.dot_general(kic, qi2_ref[p], _NT, preferred_element_type=F32)
              for p in range(IDX_HEADS // 2)]
        s = jnp.zeros((KC, KC), F32)
        for p in range(IDX_HEADS // 2):
            s = s + jnp.maximum(ds[p][:, :KC], 0.0) * w_rows[2 * p:2 * p + 1, :]
            s = s + jnp.maximum(ds[p][:, KC:], 0.0) * w_rows[2 * p + 1:2 * p + 2, :]
        score_ref[c] = jnp.where(key + c * KC < n_adm_b, s, -jnp.inf)
        return 0

    lax.fori_loop(0, nch, score_chunk, 0)

    def fold(tile, op):
        return op(tile.reshape(KC // 8, 8, KC), axis=0)

    def count(pred, x):
        xb = jnp.broadcast_to(x, (KC, KC))
        c = lax.fori_loop(0, nch, lambda c, acc: acc + fold(pred(score_ref[c], xb).astype(F32), jnp.sum),
                          jnp.zeros((8, KC), F32))
        return jnp.sum(c, axis=0, keepdims=True)

    def minmax_body(c, st):
        s = score_ref[c]
        return (jnp.minimum(st[0], fold(jnp.where(s == -jnp.inf, jnp.inf, s), jnp.min)),
                jnp.maximum(st[1], fold(s, jnp.max)))

    mn, mx = lax.fori_loop(0, nch, minmax_body,
                           (jnp.full((8, KC), jnp.inf, F32), jnp.full((8, KC), -jnp.inf, F32)))
    lo0 = jnp.min(mn, axis=0, keepdims=True)
    hi0 = jnp.max(mx, axis=0, keepdims=True)

    def midpoint(lo, hi):
        return 0.5 * lo + 0.5 * hi

    def bisect_cond(st):
        it, lo, hi, clo = st
        mid = midpoint(lo, hi)
        open_ = (clo > keep) & (mid > lo) & (mid < hi)
        return (it < MAX_BISECT) & (jnp.max(open_.astype(F32)) > 0.0)

    def bisect_body(st):
        it, lo, hi, clo = st
        for _ in range(BISECT_STEPS_PER_CHECK):
            mid = midpoint(lo, hi)
            c = count(lambda s, x: s >= x, mid)
            ge = c >= keep
            lo, hi, clo = jnp.where(ge, mid, lo), jnp.where(ge, hi, mid), jnp.where(ge, c, clo)
        return it + BISECT_STEPS_PER_CHECK, lo, hi, clo

    _, thr, _, cthr = lax.while_loop(bisect_cond, bisect_body,
                                     (jnp.int32(0), lo0, hi0, n_adm.astype(F32)))
    thr_b = jnp.broadcast_to(thr, (KC, KC))
    tied = jnp.max(cthr - keep) > 0.0

    @pl.when(jnp.logical_not(tied))
    def _():
        def select_chunk(c, _):
            sel_ref[c] = jnp.where(score_ref[c] >= thr_b, 0.0, MASKED)
            return 0

        lax.fori_loop(0, nch, select_chunk, 0)

    @pl.when(tied)
    def _():
        room = jnp.broadcast_to(keep - count(lambda s, x: s > x, thr), (KC, KC))
        earlier = (lax.broadcasted_iota(jnp.int32, (KC, KC), 1) < key).astype(BF16)

        def select_chunk(c, seen):
            s = score_ref[c]
            eq = s == thr_b
            rank = jnp.dot(earlier, eq.astype(BF16), preferred_element_type=F32) + seen
            sel_ref[c] = jnp.where((s > thr_b) | (eq & (rank < room)), 0.0, MASKED)
            return seen + jnp.sum(eq.astype(F32), axis=0, keepdims=True)

        lax.fori_loop(0, nch, select_chunk, jnp.zeros((1, KC), F32))

    cols = DSA_GROUP * KC
    kv_cols = [slice(n * HEAD_DIM, (n + 1) * HEAD_DIM) for n in range(DSA_KV_HEADS)]
    for n in range(DSA_KV_HEADS):
        qn_ref[n] = jnp.concatenate(
            [q_ref[:, (n * DSA_GROUP + g) * HEAD_DIM:(n * DSA_GROUP + g + 1) * HEAD_DIM]
             for g in range(DSA_GROUP)], axis=0)
    acc_ref[...] = jnp.zeros_like(acc_ref)

    def attend(c, st):
        kstart = pl.multiple_of(c * KC, KC)
        behind = [[jnp.clip(qblocks * (tile - c) + qb - kb, 0, 2) for qb in range(qblocks)]
                  for kb in range(qblocks)]
        sel = jnp.concatenate([sel_ref[c]] * DSA_GROUP, axis=1)
        raw = [lax.dot_general(k_ref[pl.ds(kstart, KC), kv_cols[n]], qn_ref[n], _NT,
                               preferred_element_type=F32) for n in range(DSA_KV_HEADS)]
        probs, alphas, out = [], [], []
        for n in range(DSA_KV_HEADS):
            m, l = st[2 * n], st[2 * n + 1]
            heads = [slice((n * DSA_GROUP + g) * BLK, (n * DSA_GROUP + g + 1) * BLK) for g in range(DSA_GROUP)]
            bias = jnp.concatenate(
                [jnp.concatenate([bias_ref[behind[kb][qb], :, hs] for hs in heads for qb in range(qblocks)],
                                 axis=1) for kb in range(qblocks)], axis=0)
            s = raw[n] * (HEAD_DIM ** -0.5 * LOG2_E) + bias + sel
            m_new = jnp.maximum(m, jnp.max(s, axis=0, keepdims=True))
            alpha = jnp.exp2(m - m_new)
            p = jnp.exp2(s - m_new)
            probs.append(p.astype(BF16))
            alphas.append(alpha)
            out += [m_new, alpha * l + jnp.sum(p, axis=0, keepdims=True)]
        pvs = [jnp.dot(vt_ref[c, kv_cols[n], :], probs[n], preferred_element_type=F32)
               for n in range(DSA_KV_HEADS)]
        for n in range(DSA_KV_HEADS):
            acc_ref[n] = alphas[n] * acc_ref[n] + pvs[n]
        return tuple(out)

    stats = lax.fori_loop(
        0, nch, attend,
        (jnp.full((1, cols), MASKED, F32), jnp.zeros((1, cols), F32)) * DSA_KV_HEADS)
    for n in range(DSA_KV_HEADS):
        o = acc_ref[n] / stats[2 * n + 1]
        for g in range(DSA_GROUP):
            h = n * DSA_GROUP + g
            o_ref[:, h * HEAD_DIM:(h + 1) * HEAD_DIM] = o[:, g * KC:(g + 1) * KC].T.astype(o_ref.dtype)


def dsa_attention(q, k, vt, qi, ki, wit, bias_tiles):
    b, s, _ = q.shape
    nc = s // KC
    topk = min(IDX_TOPK_MAX, s // 4)
    return pl.pallas_call(
        functools.partial(_dsa_attention_kernel, topk),
        grid=(b, nc),
        in_specs=[pl.BlockSpec((None, KC, DSA_Q), lambda bi, i: (bi, i, 0)),
                  pl.BlockSpec((None, s, DSA_KV), lambda bi, i: (bi, 0, 0)),
                  pl.BlockSpec((None, nc, DSA_KV, KC), lambda bi, i: (bi, 0, 0, 0)),
                  pl.BlockSpec((None, KC, DSA_QI), lambda bi, i: (bi, i, 0)),
                  pl.BlockSpec((None, s, IDX_HEAD_DIM), lambda bi, i: (bi, 0, 0)),
                  pl.BlockSpec((None, KC // BLK, IDX_HEADS, BLK), lambda bi, i: (bi, i, 0, 0)),
                  pl.BlockSpec((3, BLK, DSA_HEADS * BLK), lambda bi, i: (0, 0, 0))],
        out_specs=pl.BlockSpec((None, KC, DSA_Q), lambda bi, i: (bi, i, 0)),
        out_shape=jax.ShapeDtypeStruct((b, s, DSA_Q), BF16),
        scratch_shapes=[pltpu.VMEM((nc, KC, KC), F32), pltpu.VMEM((nc, KC, KC), F32),
                        pltpu.VMEM((DSA_KV_HEADS, HEAD_DIM, DSA_GROUP * KC), F32),
                        pltpu.VMEM((DSA_KV_HEADS, DSA_GROUP * KC, HEAD_DIM), BF16),
                        pltpu.VMEM((IDX_HEADS // 2, 2 * KC, IDX_HEAD_DIM), BF16)],
        compiler_params=pltpu.CompilerParams(dimension_semantics=("parallel", "arbitrary")),
        name="dsa_attention",
    )(q, k, vt, qi, ki, wit, bias_tiles)


def kernel(x, norm_mix, w_in_a, w_out_a, w_in_b, w_out_b, q_norm_b, k_norm_b, rel_bias,
           norm_mlp, w_up, w_down):
    b, s, d = x.shape
    n = b * s
    nb = s // BLK
    depth = norm_mix.shape[0]
    w_in_b = jnp.pad(w_in_b, ((0, 0), (0, 0), (0, DSA_IN_PADDED - DSA_IN)))
    w_in_a, w_out_a, w_in_b, w_out_b, w_up, w_down = (
        w.astype(BF16) for w in (w_in_a, w_out_a, w_in_b, w_out_b, w_up, w_down))
    bias_tiles = rel_bias_tiles(rel_bias)

    x = x.reshape(n, d)
    for layer in range(depth):
        j = layer // 2
        if layer % 2 == 0:
            qkv = norm_matmul(x, norm_mix[layer], w_in_a, j, BF16)
            o = stick_breaking_attention(qkv.reshape(b, s, -1))
            x = matmul_residual(o.reshape(n, -1), w_out_a, j, x)
        else:
            proj = norm_matmul(x, norm_mix[layer], w_in_b, j, F32, tn=DSA_IN_PADDED // 3)
            q, k, vt, qi, ki, wit = dsa_prep(proj, q_norm_b[j], k_norm_b[j])
            o = dsa_attention(q.reshape(b, s, -1), k.reshape(b, s, -1),
                              vt.reshape(b, s // KC, DSA_KV, KC), qi.reshape(b, s, -1),
                              ki.reshape(b, s, -1), wit.reshape(b, nb, IDX_HEADS, BLK), bias_tiles)
            x = matmul_residual(o.reshape(n, -1), w_out_b, j, x)
        x = mlp_residual(x, norm_mlp[layer], w_up, w_down, layer)
    return x.reshape(b, s, d)
```

```python
import functools
import math

import jax
import jax.numpy as jnp
import numpy as np
from jax import lax
from jax.experimental import pallas as pl
from jax.experimental.pallas import tpu as pltpu

F32 = jnp.float32
BF16 = jnp.bfloat16

RMS_EPS = 1e-6
NORM_STRIP = 128
CHUNK = 64
BLK = 128
KC = 2 * BLK
HEAD_DIM = 128
SB_HEADS = 16
SB_QBLOCKS = 2
DSA_HEADS = 16
DSA_KV_HEADS = 4
DSA_GROUP = DSA_HEADS // DSA_KV_HEADS
IDX_HEADS = 16
IDX_HEAD_DIM = 64
IDX_TOPK_MAX = 256
REL_BUCKETS = 32
REL_MAX_DIST = 128
MASKED = -1e30
LOG2_E = math.log2(math.e)
MAX_BISECT = 320
BISECT_STEPS_PER_CHECK = 4
EXP_IS_ZERO_BELOW = -104.0

_NT = (((1,), (1,)), ((), ()))


def _rms_rows(x, g):
    return x * lax.rsqrt(jnp.mean(x * x, axis=-1, keepdims=True) + RMS_EPS) * g


def _normalize_rows_into(x_ref, g_ref, xn_ref):
    def strip(r, _):
        rows = pl.ds(pl.multiple_of(r * NORM_STRIP, NORM_STRIP), NORM_STRIP)
        xn_ref[rows, :] = _rms_rows(x_ref[rows, :], g_ref[...]).astype(BF16)
        return 0

    lax.fori_loop(0, x_ref.shape[0] // NORM_STRIP, strip, 0)


def _norm_matmul_kernel(x_ref, g_ref, w_ref, o_ref, xn_ref):
    @pl.when(pl.program_id(1) == 0)
    def _():
        _normalize_rows_into(x_ref, g_ref, xn_ref)

    o_ref[...] = jnp.dot(xn_ref[...], w_ref[...], preferred_element_type=F32).astype(o_ref.dtype)


def norm_matmul(x, g, w, layer, out_dtype, tm=1024, tn=1024):
    n, d = x.shape
    m = w.shape[2]
    return pl.pallas_call(
        _norm_matmul_kernel,
        grid=(n // tm, m // tn),
        in_specs=[pl.BlockSpec((tm, d), lambda i, j: (i, 0)),
                  pl.BlockSpec((1, d), lambda i, j: (0, 0)),
                  pl.BlockSpec((None, d, tn), lambda i, j: (layer, 0, j))],
        out_specs=pl.BlockSpec((tm, tn), lambda i, j: (i, j)),
        out_shape=jax.ShapeDtypeStruct((n, m), out_dtype),
        scratch_shapes=[pltpu.VMEM((tm, d), BF16)],
        compiler_params=pltpu.CompilerParams(dimension_semantics=("parallel", "arbitrary")),
        name="norm_matmul",
    )(x, g.reshape(1, d), w)


def _matmul_residual_kernel(a_ref, w_ref, r_ref, o_ref):
    o_ref[...] = r_ref[...] + jnp.dot(a_ref[...], w_ref[...], preferred_element_type=F32)


def matmul_residual(a, w, layer, res, tm=512):
    n, k = a.shape
    m = w.shape[2]
    return pl.pallas_call(
        _matmul_residual_kernel,
        grid=(n // tm,),
        in_specs=[pl.BlockSpec((tm, k), lambda i: (i, 0)),
                  pl.BlockSpec((None, k, m), lambda i: (layer, 0, 0)),
                  pl.BlockSpec((tm, m), lambda i: (i, 0))],
        out_specs=pl.BlockSpec((tm, m), lambda i: (i, 0)),
        out_shape=jax.ShapeDtypeStruct((n, m), F32),
        compiler_params=pltpu.CompilerParams(dimension_semantics=("parallel",)),
        name="matmul_residual",
    )(a, w, res)


def _mlp_kernel(x_ref, g_ref, wu_ref, wd_ref, o_ref, xn_ref):
    f = pl.program_id(1)

    @pl.when(f == 0)
    def _():
        _normalize_rows_into(x_ref, g_ref, xn_ref)
        o_ref[...] = x_ref[...]

    h = jnp.dot(xn_ref[...], wu_ref[...], preferred_element_type=F32)
    a = jnp.square(jnp.maximum(h, 0.0)).astype(BF16)
    o_ref[...] += jnp.dot(a, wd_ref[...], preferred_element_type=F32)


def mlp_residual(x, g, w_up, w_down, layer, tm=512, tf=1024):
    n, d = x.shape
    dff = w_up.shape[2]
    return pl.pallas_call(
        _mlp_kernel,
        grid=(n // tm, dff // tf),
        in_specs=[pl.BlockSpec((tm, d), lambda i, f: (i, 0)),
                  pl.BlockSpec((1, d), lambda i, f: (0, 0)),
                  pl.BlockSpec((None, d, tf), lambda i, f: (layer, 0, f)),
                  pl.BlockSpec((None, tf, d), lambda i, f: (layer, f, 0))],
        out_specs=pl.BlockSpec((tm, d), lambda i, f: (i, 0)),
        out_shape=jax.ShapeDtypeStruct((n, d), F32),
        scratch_shapes=[pltpu.VMEM((tm, d), BF16)],
        compiler_params=pltpu.CompilerParams(dimension_semantics=("parallel", "arbitrary")),
        name="mlp_residual",
    )(x, g.reshape(1, d), w_up, w_down)


def _stick_breaking_kernel(q_ref, k_ref, v_ref, o_ref, acc_ref):
    tile = pl.program_id(2)
    heads = q_ref.shape[1] // HEAD_DIM
    scale = HEAD_DIM ** -0.5
    row = lax.broadcasted_iota(jnp.int32, (BLK, BLK), 0)
    col = lax.broadcasted_iota(jnp.int32, (BLK, BLK), 1)
    later = (row > col).astype(BF16)
    later2 = jnp.concatenate([later, later], axis=0)
    causal = col < row
    chains = [(slice(qb * BLK, (qb + 1) * BLK), slice(g * HEAD_DIM, (g + 1) * HEAD_DIM))
              for qb in range(SB_QBLOCKS) for g in range(heads)]

    def step(back, carries, diagonal):
        first = tile * SB_QBLOCKS - back
        if not diagonal:
            carries = [jnp.where(first + n // heads >= 0, c, MASKED) for n, c in enumerate(carries)]
        starts = [pl.multiple_of(jnp.maximum(first + qb, 0) * BLK, BLK) for qb in range(SB_QBLOCKS)]
        zs = [lax.dot_general(q_ref[rows, cs], k_ref[pl.ds(starts[n // heads], BLK), cs], _NT,
                              preferred_element_type=F32) for n, (rows, cs) in enumerate(chains)]
        log_bs, log_1mbs, splits = [], [], []
        for z in zs:
            z = z * scale
            log_b = jnp.minimum(z, 0.0) - jnp.log(1.0 + jnp.exp(-jnp.abs(z)))
            log_1mb = log_b - z
            if diagonal:
                log_1mb = jnp.where(causal, log_1mb, 0.0)
            hi = log_1mb.astype(BF16)
            lo = (log_1mb - hi.astype(F32)).astype(BF16)
            log_bs.append(log_b)
            log_1mbs.append(log_1mb)
            splits.append(jnp.concatenate([hi, lo], axis=1))
        tails = jnp.dot(jnp.concatenate(splits, axis=0), later2, preferred_element_type=F32)
        weights = []
        for n in range(len(chains)):
            a = jnp.exp(log_bs[n] + tails[n * BLK:(n + 1) * BLK] + carries[n])
            if diagonal:
                a = jnp.where(causal, a, 0.0)
            weights.append(a.astype(BF16))
        contribs = [jnp.dot(weights[n], v_ref[pl.ds(starts[n // heads], BLK), cs],
                            preferred_element_type=F32) for n, (_, cs) in enumerate(chains)]
        for n, (rows, cs) in enumerate(chains):
            if diagonal:
                acc_ref[rows, cs] = contribs[n]
            else:
                acc_ref[rows, cs] += contribs[n]
        return tuple(carries[n] + jnp.sum(log_1mbs[n], axis=1, keepdims=True) for n in range(len(chains)))

    carries = step(0, (jnp.zeros((BLK, 1), F32),) * len(chains), True)

    def cond(state):
        live = functools.reduce(jnp.maximum, state[1:])
        return (state[0] <= tile * SB_QBLOCKS + SB_QBLOCKS - 1) & (jnp.max(live) > EXP_IS_ZERO_BELOW)

    def body(state):
        return (state[0] + 1,) + step(state[0], list(state[1:]), False)

    lax.while_loop(cond, body, (jnp.int32(1),) + carries)
    o_ref[...] = acc_ref[...].astype(o_ref.dtype)


def stick_breaking_attention(qkv, heads_per_step=8):
    b, s, _ = qkv.shape
    groups = SB_HEADS // heads_per_step
    w = heads_per_step * HEAD_DIM
    tq = SB_QBLOCKS * BLK
    return pl.pallas_call(
        _stick_breaking_kernel,
        grid=(b, groups, s // tq),
        in_specs=[pl.BlockSpec((None, tq, w), lambda bi, gi, i: (bi, i, gi)),
                  pl.BlockSpec((None, s, w), lambda bi, gi, i: (bi, 0, groups + gi)),
                  pl.BlockSpec((None, s, w), lambda bi, gi, i: (bi, 0, 2 * groups + gi))],
        out_specs=pl.BlockSpec((None, tq, w), lambda bi, gi, i: (bi, i, gi)),
        out_shape=jax.ShapeDtypeStruct((b, s, SB_HEADS * HEAD_DIM), BF16),
        scratch_shapes=[pltpu.VMEM((tq, w), F32)],
        compiler_params=pltpu.CompilerParams(
            dimension_semantics=("parallel", "parallel", "arbitrary")),
        name="stick_breaking_attention",
    )(qkv, qkv, qkv)


DSA_Q = DSA_HEADS * HEAD_DIM
DSA_KV = DSA_KV_HEADS * HEAD_DIM
DSA_QI = IDX_HEADS * IDX_HEAD_DIM
DSA_SPLITS = (0, DSA_Q, DSA_Q + DSA_KV, DSA_Q + 2 * DSA_KV, DSA_Q + 2 * DSA_KV + DSA_QI,
              DSA_Q + 2 * DSA_KV + DSA_QI + IDX_HEAD_DIM,
              DSA_Q + 2 * DSA_KV + DSA_QI + IDX_HEAD_DIM + IDX_HEADS)
DSA_IN = DSA_SPLITS[-1]
DSA_IN_PADDED = -(-DSA_IN // 128) * 128
assert DSA_SPLITS[4] % 128 == 0 and DSA_IN_PADDED - DSA_SPLITS[4] == 128 and IDX_HEAD_DIM % 8 == 0


def _dsa_prep_kernel(p_ref, qg_ref, kg_ref, q_ref, k_ref, vt_ref, qi_ref, ki_ref, wit_ref):
    s = DSA_SPLITS
    for h in range(DSA_HEADS):
        lo = s[0] + h * HEAD_DIM
        q_ref[:, h * HEAD_DIM:(h + 1) * HEAD_DIM] = _rms_rows(
            p_ref[:, lo:lo + HEAD_DIM], qg_ref[...]).astype(BF16)
    for h in range(DSA_KV_HEADS):
        lo = s[1] + h * HEAD_DIM
        k_ref[:, h * HEAD_DIM:(h + 1) * HEAD_DIM] = _rms_rows(
            p_ref[:, lo:lo + HEAD_DIM], kg_ref[...]).astype(BF16)
        lo = s[2] + h * HEAD_DIM
        vt_ref[h * HEAD_DIM:(h + 1) * HEAD_DIM, :] = p_ref[:, lo:lo + HEAD_DIM].T.astype(BF16)
    qi_ref[...] = p_ref[:, s[3]:s[4]].astype(BF16)
    ki_ref[...] = p_ref[:, s[4]:s[5]].astype(BF16)
    tail_t = p_ref[:, s[4]:DSA_IN_PADDED].T
    w_t = tail_t[IDX_HEAD_DIM:IDX_HEAD_DIM + IDX_HEADS, :] * (IDX_HEADS ** -0.5 * IDX_HEAD_DIM ** -0.5)
    for e in range(KC // BLK):
        wit_ref[e] = w_t[:, e * BLK:(e + 1) * BLK]


def dsa_prep(proj, q_gain, k_gain):
    n = proj.shape[0]
    nc = n // KC
    row = lambda w: pl.BlockSpec((KC, w), lambda i: (i, 0))
    return pl.pallas_call(
        _dsa_prep_kernel,
        grid=(nc,),
        in_specs=[row(DSA_IN_PADDED),
                  pl.BlockSpec((1, HEAD_DIM), lambda i: (0, 0)),
                  pl.BlockSpec((1, HEAD_DIM), lambda i: (0, 0))],
        out_specs=[row(DSA_Q), row(DSA_KV),
                   pl.BlockSpec((None, DSA_KV, KC), lambda i: (i, 0, 0)),
                   row(DSA_QI), row(IDX_HEAD_DIM),
                   pl.BlockSpec((KC // BLK, IDX_HEADS, BLK), lambda i: (i, 0, 0))],
        out_shape=[jax.ShapeDtypeStruct((n, DSA_Q), BF16),
                   jax.ShapeDtypeStruct((n, DSA_KV), BF16),
                   jax.ShapeDtypeStruct((nc, DSA_KV, KC), BF16),
                   jax.ShapeDtypeStruct((n, DSA_QI), BF16),
                   jax.ShapeDtypeStruct((n, IDX_HEAD_DIM), BF16),
                   jax.ShapeDtypeStruct((n // BLK, IDX_HEADS, BLK), F32)],
        compiler_params=pltpu.CompilerParams(dimension_semantics=("parallel",)),
        name="dsa_prep",
    )(proj, q_gain.reshape(1, HEAD_DIM), k_gain.reshape(1, HEAD_DIM))


def _t5_bucket_tiles():
    half = REL_BUCKETS // 2
    max_exact = half // 2
    s = np.arange(BLK)[:, None]
    t = np.arange(BLK)[None, :]
    tiles = []
    for behind in range(3):
        rel = s - t - behind * BLK
        n = np.abs(rel)
        nf = np.maximum(n, max_exact).astype(np.float64)
        large = max_exact + (np.log(nf / max_exact) / math.log(REL_MAX_DIST / max_exact)
                             * (half - max_exact)).astype(np.int32)
        large = np.minimum(large, half - 1)
        tiles.append(np.where(rel > 0, half, 0) + np.where(n < max_exact, n, large))
    return np.stack(tiles).astype(np.int32)


def _rel_bias_tiles_kernel(rb_ref, bucket_ref, o_ref):
    h = pl.program_id(1)
    bucket = bucket_ref[...]
    acc = jnp.zeros((BLK, BLK), F32)
    for b in range(REL_BUCKETS):
        acc = jnp.where(bucket == b, rb_ref[h, b], acc)
    o_ref[...] = acc * LOG2_E


def rel_bias_tiles(rel_bias):
    return pl.pallas_call(
        _rel_bias_tiles_kernel,
        grid=(3, DSA_HEADS),
        in_specs=[pl.BlockSpec(memory_space=pltpu.SMEM),
                  pl.BlockSpec((None, BLK, BLK), lambda d, h: (d, 0, 0))],
        out_specs=pl.BlockSpec((None, BLK, BLK), lambda d, h: (d, 0, h)),
        out_shape=jax.ShapeDtypeStruct((3, BLK, DSA_HEADS * BLK), F32),
        name="rel_bias_tiles",
    )(rel_bias, jnp.asarray(_t5_bucket_tiles()))


def _dsa_attention_kernel(topk, q_ref, k_ref, vt_ref, qi_ref, ki_ref, wit_ref, bias_ref,
                          o_ref, score_ref, sel_ref, acc_ref, qn_ref, qi2_ref):
    tile = pl.program_id(1)
    nch = tile + 1
    qblocks = KC // BLK
    key = lax.broadcasted_iota(jnp.int32, (KC, KC), 0)
    t_local = lax.broadcasted_iota(jnp.int32, (1, KC), 1)
    n_adm = tile * KC + (t_local // CHUNK + 1) * CHUNK
    n_adm_b = jnp.broadcast_to(n_adm, (KC, KC))
    block_end = tile * KC + (t_local // BLK + 1) * BLK
    keep = jnp.minimum(jnp.minimum(topk, block_end), n_adm).astype(F32)

    for p in range(IDX_HEADS // 2):
        qi2_ref[p] = jnp.concatenate(
            [qi_ref[:, (2 * p + e) * IDX_HEAD_DIM:(2 * p + e + 1) * IDX_HEAD_DIM] for e in range(2)], axis=0)
    w_rows = jnp.concatenate([wit_ref[e] for e in range(qblocks)], axis=1)

    def score_chunk(c, _):
        kic = ki_ref[pl.ds(pl.multiple_of(c * KC, KC), KC), :]
        ds = [lax.dot_general(kic, qi2_ref[p], _NT, preferred_element_type=F32)
              for p in range(IDX_HEADS // 2)]
        s = jnp.zeros((KC, KC), F32)
        for p in range(IDX_HEADS // 2):
            s = s + jnp.maximum(ds[p][:, :KC], 0.0) * w_rows[2 * p:2 * p + 1, :]
            s = s + jnp.maximum(ds[p][:, KC:], 0.0) * w_rows[2 * p + 1:2 * p + 2, :]
        score_ref[c] = jnp.where(key + c * KC < n_adm_b, s, -jnp.inf)
        return 0

    lax.fori_loop(0, nch, score_chunk, 0)

    def fold(tile, op):
        return op(tile.reshape(KC // 8, 8, KC), axis=0)

    def count(pred, x):
        xb = jnp.broadcast_to(x, (KC, KC))
        c = lax.fori_loop(0, nch, lambda c, acc: acc + fold(pred(score_ref[c], xb).astype(F32), jnp.sum),
                          jnp.zeros((8, KC), F32))
        return jnp.sum(c, axis=0, keepdims=True)

    def minmax_body(c, st):
        s = score_ref[c]
        return (jnp.minimum(st[0], fold(jnp.where(s == -jnp.inf, jnp.inf, s), jnp.min)),
                jnp.maximum(st[1], fold(s, jnp.max)))

    mn, mx = lax.fori_loop(0, nch, minmax_body,
                           (jnp.full((8, KC), jnp.inf, F32), jnp.full((8, KC), -jnp.inf, F32)))
    lo0 = jnp.min(mn, axis=0, keepdims=True)
    hi0 = jnp.max(mx, axis=0, keepdims=True)

    def midpoint(lo, hi):
        return 0.5 * lo + 0.5 * hi

    def bisect_cond(st):
        it, lo, hi, clo = st
        mid = midpoint(lo, hi)
        open_ = (clo > keep) & (mid > lo) & (mid < hi)
        return (it < MAX_BISECT) & (jnp.max(open_.astype(F32)) > 0.0)

    def bisect_body(st):
        it, lo, hi, clo = st
        for _ in range(BISECT_STEPS_PER_CHECK):
            mid = midpoint(lo, hi)
            c = count(lambda s, x: s >= x, mid)
            ge = c >= keep
            lo, hi, clo = jnp.where(ge, mid, lo), jnp.where(ge, hi, mid), jnp.where(ge, c, clo)
        return it + BISECT_STEPS_PER_CHECK, lo, hi, clo

    _, thr, _, cthr = lax.while_loop(bisect_cond, bisect_body,
                                     (jnp.int32(0), lo0, hi0, n_adm.astype(F32)))
    thr_b = jnp.broadcast_to(thr, (KC, KC))
    tied = jnp.max(cthr - keep) > 0.0

    @pl.when(jnp.logical_not(tied))
    def _():
        def select_chunk(c, _):
            sel_ref[c] = jnp.where(score_ref[c] >= thr_b, 0.0, MASKED)
            return 0

        lax.fori_loop(0, nch, select_chunk, 0)

    @pl.when(tied)
    def _():
        room = jnp.broadcast_to(keep - count(lambda s, x: s > x, thr), (KC, KC))
        earlier = (lax.broadcasted_iota(jnp.int32, (KC, KC), 1) < key).astype(BF16)

        def select_chunk(c, seen):
            s = score_ref[c]
            eq = s == thr_b
            rank = jnp.dot(earlier, eq.astype(BF16), preferred_element_type=F32) + seen
            sel_ref[c] = jnp.where((s > thr_b) | (eq & (rank < room)), 0.0, MASKED)
            return seen + jnp.sum(eq.astype(F32), axis=0, keepdims=True)

        lax.fori_loop(0, nch, select_chunk, jnp.zeros((1, KC), F32))

    cols = DSA_GROUP * KC
    kv_cols = [slice(n * HEAD_DIM, (n + 1) * HEAD_DIM) for n in range(DSA_KV_HEADS)]
    for n in range(DSA_KV_HEADS):
        qn_ref[n] = jnp.concatenate(
            [q_ref[:, (n * DSA_GROUP + g) * HEAD_DIM:(n * DSA_GROUP + g + 1) * HEAD_DIM]
             for g in range(DSA_GROUP)], axis=0)
    acc_ref[...] = jnp.zeros_like(acc_ref)

    def attend(c, st):
        kstart = pl.multiple_of(c * KC, KC)
        behind = [[jnp.clip(qblocks * (tile - c) + qb - kb, 0, 2) for qb in range(qblocks)]
                  for kb in range(qblocks)]
        sel = jnp.concatenate([sel_ref[c]] * DSA_GROUP, axis=1)
        raw = [lax.dot_general(k_ref[pl.ds(kstart, KC), kv_cols[n]], qn_ref[n], _NT,
                               preferred_element_type=F32) for n in range(DSA_KV_HEADS)]
        probs, alphas, out = [], [], []
        for n in range(DSA_KV_HEADS):
            m, l = st[2 * n], st[2 * n + 1]
            heads = [slice((n * DSA_GROUP + g) * BLK, (n * DSA_GROUP + g + 1) * BLK) for g in range(DSA_GROUP)]
            bias = jnp.concatenate(
                [jnp.concatenate([bias_ref[behind[kb][qb], :, hs] for hs in heads for qb in range(qblocks)],
                                 axis=1) for kb in range(qblocks)], axis=0)
            s = raw[n] * (HEAD_DIM ** -0.5 * LOG2_E) + bias + sel
            m_new = jnp.maximum(m, jnp.max(s, axis=0, keepdims=True))
            alpha = jnp.exp2(m - m_new)
            p = jnp.exp2(s - m_new)
            probs.append(p.astype(BF16))
            alphas.append(alpha)
            out += [m_new, alpha * l + jnp.sum(p, axis=0, keepdims=True)]
        pvs = [jnp.dot(vt_ref[c, kv_cols[n], :], probs[n], preferred_element_type=F32)
               for n in range(DSA_KV_HEADS)]
        for n in range(DSA_KV_HEADS):
            acc_ref[n] = alphas[n] * acc_ref[n] + pvs[n]
        return tuple(out)

    stats = lax.fori_loop(
        0, nch, attend,
        (jnp.full((1, cols), MASKED, F32), jnp.zeros((1, cols), F32)) * DSA_KV_HEADS)
    for n in range(DSA_KV_HEADS):
        o = acc_ref[n] / stats[2 * n + 1]
        for g in range(DSA_GROUP):
            h = n * DSA_GROUP + g
            o_ref[:, h * HEAD_DIM:(h + 1) * HEAD_DIM] = o[:, g * KC:(g + 1) * KC].T.astype(o_ref.dtype)


def dsa_attention(q, k, vt, qi, ki, wit, bias_tiles):
    b, s, _ = q.shape
    nc = s // KC
    topk = min(IDX_TOPK_MAX, s // 4)
    return pl.pallas_call(
        functools.partial(_dsa_attention_kernel, topk),
        grid=(b, nc),
        in_specs=[pl.BlockSpec((None, KC, DSA_Q), lambda bi, i: (bi, i, 0)),
                  pl.BlockSpec((None, s, DSA_KV), lambda bi, i: (bi, 0, 0)),
                  pl.BlockSpec((None, nc, DSA_KV, KC), lambda bi, i: (bi, 0, 0, 0)),
                  pl.BlockSpec((None, KC, DSA_QI), lambda bi, i: (bi, i, 0)),
                  pl.BlockSpec((None, s, IDX_HEAD_DIM), lambda bi, i: (bi, 0, 0)),
                  pl.BlockSpec((None, KC // BLK, IDX_HEADS, BLK), lambda bi, i: (bi, i, 0, 0)),
                  pl.BlockSpec((3, BLK, DSA_HEADS * BLK), lambda bi, i: (0, 0, 0))],
        out_specs=pl.BlockSpec((None, KC, DSA_Q), lambda bi, i: (bi, i, 0)),
        out_shape=jax.ShapeDtypeStruct((b, s, DSA_Q), BF16),
        scratch_shapes=[pltpu.VMEM((nc, KC, KC), F32), pltpu.VMEM((nc, KC, KC), F32),
                        pltpu.VMEM((DSA_KV_HEADS, HEAD_DIM, DSA_GROUP * KC), F32),
                        pltpu.VMEM((DSA_KV_HEADS, DSA_GROUP * KC, HEAD_DIM), BF16),
                        pltpu.VMEM((IDX_HEADS // 2, 2 * KC, IDX_HEAD_DIM), BF16)],
        compiler_params=pltpu.CompilerParams(dimension_semantics=("parallel", "arbitrary")),
        name="dsa_attention",
    )(q, k, vt, qi, ki, wit, bias_tiles)


def kernel(x, norm_mix, w_in_a, w_out_a, w_in_b, w_out_b, q_norm_b, k_norm_b, rel_bias,
           norm_mlp, w_up, w_down):
    b, s, d = x.shape
    n = b * s
    nb = s // BLK
    depth = norm_mix.shape[0]
    w_in_b = jnp.pad(w_in_b, ((0, 0), (0, 0), (0, DSA_IN_PADDED - DSA_IN)))
    w_in_a, w_out_a, w_in_b, w_out_b, w_up, w_down = (
        w.astype(BF16) for w in (w_in_a, w_out_a, w_in_b, w_out_b, w_up, w_down))
    bias_tiles = rel_bias_tiles(rel_bias)

    x = x.reshape(n, d)
    for layer in range(depth):
        j = layer // 2
        if layer % 2 == 0:
            qkv = norm_matmul(x, norm_mix[layer], w_in_a, j, BF16)
            o = stick_breaking_attention(qkv.reshape(b, s, -1))
            x = matmul_residual(o.reshape(n, -1), w_out_a, j, x)
        else:
            proj = norm_matmul(x, norm_mix[layer], w_in_b, j, F32, tn=DSA_IN_PADDED // 3)
            q, k, vt, qi, ki, wit = dsa_prep(proj, q_norm_b[j], k_norm_b[j])
            o = dsa_attention(q.reshape(b, s, -1), k.reshape(b, s, -1),
                              vt.reshape(b, s // KC, DSA_KV, KC), qi.reshape(b, s, -1),
                              ki.reshape(b, s, -1), wit.reshape(b, nb, IDX_HEADS, BLK), bias_tiles)
            x = matmul_residual(o.reshape(n, -1), w_out_b, j, x)
        x = mlp_residual(x, norm_mlp[layer], w_up, w_down, layer)
    return x.reshape(b, s, d)
```

```python
import functools
import math

import jax
import jax.numpy as jnp
import numpy as np
from jax import lax
from jax.experimental import pallas as pl
from jax.experimental.pallas import tpu as pltpu

F32 = jnp.float32
BF16 = jnp.bfloat16

RMS_EPS = 1e-6
NORM_STRIP = 128
CHUNK = 64
BLK = 128
KC = 2 * BLK
HEAD_DIM = 128
SB_HEADS = 16
SB_QBLOCKS = 2
DSA_HEADS = 16
DSA_KV_HEADS = 4
DSA_GROUP = DSA_HEADS // DSA_KV_HEADS
IDX_HEADS = 16
IDX_HEAD_DIM = 64
IDX_TOPK_MAX = 256
REL_BUCKETS = 32
REL_MAX_DIST = 128
MASKED = -1e30
LOG2_E = math.log2(math.e)
MAX_BISECT = 320
BISECT_STEPS_PER_CHECK = 4
EXP_IS_ZERO_BELOW = -104.0

_NT = (((1,), (1,)), ((), ()))


def _rms_rows(x, g):
    return x * lax.rsqrt(jnp.mean(x * x, axis=-1, keepdims=True) + RMS_EPS) * g


def _normalize_rows_into(x_ref, g_ref, xn_ref):
    def strip(r, _):
        rows = pl.ds(pl.multiple_of(r * NORM_STRIP, NORM_STRIP), NORM_STRIP)
        xn_ref[rows, :] = _rms_rows(x_ref[rows, :], g_ref[...]).astype(BF16)
        return 0

    lax.fori_loop(0, x_ref.shape[0] // NORM_STRIP, strip, 0)


def _norm_matmul_kernel(x_ref, g_ref, w_ref, o_ref, xn_ref):
    @pl.when(pl.program_id(1) == 0)
    def _():
        _normalize_rows_into(x_ref, g_ref, xn_ref)

    o_ref[...] = jnp.dot(xn_ref[...], w_ref[...], preferred_element_type=F32).astype(o_ref.dtype)


def norm_matmul(x, g, w, layer, out_dtype, tm=1024, tn=1024):
    n, d = x.shape
    m = w.shape[2]
    return pl.pallas_call(
        _norm_matmul_kernel,
        grid=(n // tm, m // tn),
        in_specs=[pl.BlockSpec((tm, d), lambda i, j: (i, 0)),
                  pl.BlockSpec((1, d), lambda i, j: (0, 0)),
                  pl.BlockSpec((None, d, tn), lambda i, j: (layer, 0, j))],
        out_specs=pl.BlockSpec((tm, tn), lambda i, j: (i, j)),
        out_shape=jax.ShapeDtypeStruct((n, m), out_dtype),
        scratch_shapes=[pltpu.VMEM((tm, d), BF16)],
        compiler_params=pltpu.CompilerParams(dimension_semantics=("parallel", "arbitrary")),
        name="norm_matmul",
    )(x, g.reshape(1, d), w)


def _matmul_residual_kernel(a_ref, w_ref, r_ref, o_ref):
    o_ref[...] = r_ref[...] + jnp.dot(a_ref[...], w_ref[...], preferred_element_type=F32)


def matmul_residual(a, w, layer, res, tm=512):
    n, k = a.shape
    m = w.shape[2]
    return pl.pallas_call(
        _matmul_residual_kernel,
        grid=(n // tm,),
        in_specs=[pl.BlockSpec((tm, k), lambda i: (i, 0)),
                  pl.BlockSpec((None, k, m), lambda i: (layer, 0, 0)),
                  pl.BlockSpec((tm, m), lambda i: (i, 0))],
        out_specs=pl.BlockSpec((tm, m), lambda i: (i, 0)),
        out_shape=jax.ShapeDtypeStruct((n, m), F32),
        compiler_params=pltpu.CompilerParams(dimension_semantics=("parallel",)),
        name="matmul_residual",
    )(a, w, res)


def _mlp_kernel(x_ref, g_ref, wu_ref, wd_ref, o_ref, xn_ref):
    f = pl.program_id(1)

    @pl.when(f == 0)
    def _():
        _normalize_rows_into(x_ref, g_ref, xn_ref)
        o_ref[...] = x_ref[...]

    h = jnp.dot(xn_ref[...], wu_ref[...], preferred_element_type=F32)
    a = jnp.square(jnp.maximum(h, 0.0)).astype(BF16)
    o_ref[...] += jnp.dot(a, wd_ref[...], preferred_element_type=F32)


def mlp_residual(x, g, w_up, w_down, layer, tm=512, tf=1024):
    n, d = x.shape
    dff = w_up.shape[2]
    return pl.pallas_call(
        _mlp_kernel,
        grid=(n // tm, dff // tf),
        in_specs=[pl.BlockSpec((tm, d), lambda i, f: (i, 0)),
                  pl.BlockSpec((1, d), lambda i, f: (0, 0)),
                  pl.BlockSpec((None, d, tf), lambda i, f: (layer, 0, f)),
                  pl.BlockSpec((None, tf, d), lambda i, f: (layer, f, 0))],
        out_specs=pl.BlockSpec((tm, d), lambda i, f: (i, 0)),
        out_shape=jax.ShapeDtypeStruct((n, d), F32),
        scratch_shapes=[pltpu.VMEM((tm, d), BF16)],
        compiler_params=pltpu.CompilerParams(dimension_semantics=("parallel", "arbitrary")),
        name="mlp_residual",
    )(x, g.reshape(1, d), w_up, w_down)


def _stick_breaking_kernel(q_ref, k_ref, v_ref, o_ref, acc_ref):
    tile = pl.program_id(2)
    heads = q_ref.shape[1] // HEAD_DIM
    scale = HEAD_DIM ** -0.5
    row = lax.broadcasted_iota(jnp.int32, (BLK, BLK), 0)
    col = lax.broadcasted_iota(jnp.int32, (BLK, BLK), 1)
    later = (row > col).astype(BF16)
    later2 = jnp.concatenate([later, later], axis=0)
    causal = col < row
    chains = [(slice(qb * BLK, (qb + 1) * BLK), slice(g * HEAD_DIM, (g + 1) * HEAD_DIM))
              for qb in range(SB_QBLOCKS) for g in range(heads)]

    def step(back, carries, diagonal):
        first = tile * SB_QBLOCKS - back
        if not diagonal:
            carries = [jnp.where(first + n // heads >= 0, c, MASKED) for n, c in enumerate(carries)]
        starts = [pl.multiple_of(jnp.maximum(first + qb, 0) * BLK, BLK) for qb in range(SB_QBLOCKS)]
        zs = [lax.dot_general(q_ref[rows, cs], k_ref[pl.ds(starts[n // heads], BLK), cs], _NT,
                              preferred_element_type=F32) for n, (rows, cs) in enumerate(chains)]
        log_bs, log_1mbs, splits = [], [], []
        for z in zs:
            z = z * scale
            log_b = jnp.minimum(z, 0.0) - jnp.log(1.0 + jnp.exp(-jnp.abs(z)))
            log_1mb = log_b - z
            if diagonal:
                log_1mb = jnp.where(causal, log_1mb, 0.0)
            hi = log_1mb.astype(BF16)
            lo = (log_1mb - hi.astype(F32)).astype(BF16)
            log_bs.append(log_b)
            log_1mbs.append(log_1mb)
            splits.append(jnp.concatenate([hi, lo], axis=1))
        tails = jnp.dot(jnp.concatenate(splits, axis=0), later2, preferred_element_type=F32)
        weights = []
        for n in range(len(chains)):
            a = jnp.exp(log_bs[n] + tails[n * BLK:(n + 1) * BLK] + carries[n])
            if diagonal:
                a = jnp.where(causal, a, 0.0)
            weights.append(a.astype(BF16))
        contribs = [jnp.dot(weights[n], v_ref[pl.ds(starts[n // heads], BLK), cs],
                            preferred_element_type=F32) for n, (_, cs) in enumerate(chains)]
        for n, (rows, cs) in enumerate(chains):
            if diagonal:
                acc_ref[rows, cs] = contribs[n]
            else:
                acc_ref[rows, cs] += contribs[n]
        return tuple(carries[n] + jnp.sum(log_1mbs[n], axis=1, keepdims=True) for n in range(len(chains)))

    carries = step(0, (jnp.zeros((BLK, 1), F32),) * len(chains), True)

    def cond(state):
        live = functools.reduce(jnp.maximum, state[1:])
        return (state[0] <= tile * SB_QBLOCKS + SB_QBLOCKS - 1) & (jnp.max(live) > EXP_IS_ZERO_BELOW)

    def body(state):
        return (state[0] + 1,) + step(state[0], list(state[1:]), False)

    lax.while_loop(cond, body, (jnp.int32(1),) + carries)
    o_ref[...] = acc_ref[...].astype(o_ref.dtype)


def stick_breaking_attention(qkv, heads_per_step=8):
    b, s, _ = qkv.shape
    groups = SB_HEADS // heads_per_step
    w = heads_per_step * HEAD_DIM
    tq = SB_QBLOCKS * BLK
    return pl.pallas_call(
        _stick_breaking_kernel,
        grid=(b, groups, s // tq),
        in_specs=[pl.BlockSpec((None, tq, w), lambda bi, gi, i: (bi, i, gi)),
                  pl.BlockSpec((None, s, w), lambda bi, gi, i: (bi, 0, groups + gi)),
                  pl.BlockSpec((None, s, w), lambda bi, gi, i: (bi, 0, 2 * groups + gi))],
        out_specs=pl.BlockSpec((None, tq, w), lambda bi, gi, i: (bi, i, gi)),
        out_shape=jax.ShapeDtypeStruct((b, s, SB_HEADS * HEAD_DIM), BF16),
        scratch_shapes=[pltpu.VMEM((tq, w), F32)],
        compiler_params=pltpu.CompilerParams(
            dimension_semantics=("parallel", "parallel", "arbitrary")),
        name="stick_breaking_attention",
    )(qkv, qkv, qkv)


DSA_Q = DSA_HEADS * HEAD_DIM
DSA_KV = DSA_KV_HEADS * HEAD_DIM
DSA_QI = IDX_HEADS * IDX_HEAD_DIM
DSA_SPLITS = (0, DSA_Q, DSA_Q + DSA_KV, DSA_Q + 2 * DSA_KV, DSA_Q + 2 * DSA_KV + DSA_QI,
              DSA_Q + 2 * DSA_KV + DSA_QI + IDX_HEAD_DIM,
              DSA_Q + 2 * DSA_KV + DSA_QI + IDX_HEAD_DIM + IDX_HEADS)
DSA_IN = DSA_SPLITS[-1]
DSA_IN_PADDED = -(-DSA_IN // 128) * 128
assert DSA_SPLITS[4] % 128 == 0 and DSA_IN_PADDED - DSA_SPLITS[4] == 128 and IDX_HEAD_DIM % 8 == 0


def _dsa_prep_kernel(p_ref, qg_ref, kg_ref, q_ref, k_ref, vt_ref, qi_ref, ki_ref, wit_ref):
    s = DSA_SPLITS
    for h in range(DSA_HEADS):
        lo = s[0] + h * HEAD_DIM
        q_ref[:, h * HEAD_DIM:(h + 1) * HEAD_DIM] = _rms_rows(
            p_ref[:, lo:lo + HEAD_DIM], qg_ref[...]).astype(BF16)
    for h in range(DSA_KV_HEADS):
        lo = s[1] + h * HEAD_DIM
        k_ref[:, h * HEAD_DIM:(h + 1) * HEAD_DIM] = _rms_rows(
            p_ref[:, lo:lo + HEAD_DIM], kg_ref[...]).astype(BF16)
        lo = s[2] + h * HEAD_DIM
        vt_ref[h * HEAD_DIM:(h + 1) * HEAD_DIM, :] = p_ref[:, lo:lo + HEAD_DIM].T.astype(BF16)
    qi_ref[...] = p_ref[:, s[3]:s[4]].astype(BF16)
    ki_ref[...] = p_ref[:, s[4]:s[5]].astype(BF16)
    tail_t = p_ref[:, s[4]:DSA_IN_PADDED].T
    w_t = tail_t[IDX_HEAD_DIM:IDX_HEAD_DIM + IDX_HEADS, :] * (IDX_HEADS ** -0.5 * IDX_HEAD_DIM ** -0.5)
    for e in range(KC // BLK):
        wit_ref[e] = w_t[:, e * BLK:(e + 1) * BLK]


def dsa_prep(proj, q_gain, k_gain):
    n = proj.shape[0]
    nc = n // KC
    row = lambda w: pl.BlockSpec((KC, w), lambda i: (i, 0))
    return pl.pallas_call(
        _dsa_prep_kernel,
        grid=(nc,),
        in_specs=[row(DSA_IN_PADDED),
                  pl.BlockSpec((1, HEAD_DIM), lambda i: (0, 0)),
                  pl.BlockSpec((1, HEAD_DIM), lambda i: (0, 0))],
        out_specs=[row(DSA_Q), row(DSA_KV),
                   pl.BlockSpec((None, DSA_KV, KC), lambda i: (i, 0, 0)),
                   row(DSA_QI), row(IDX_HEAD_DIM),
                   pl.BlockSpec((KC // BLK, IDX_HEADS, BLK), lambda i: (i, 0, 0))],
        out_shape=[jax.ShapeDtypeStruct((n, DSA_Q), BF16),
                   jax.ShapeDtypeStruct((n, DSA_KV), BF16),
                   jax.ShapeDtypeStruct((nc, DSA_KV, KC), BF16),
                   jax.ShapeDtypeStruct((n, DSA_QI), BF16),
                   jax.ShapeDtypeStruct((n, IDX_HEAD_DIM), BF16),
                   jax.ShapeDtypeStruct((n // BLK, IDX_HEADS, BLK), F32)],
        compiler_params=pltpu.CompilerParams(dimension_semantics=("parallel",)),
        name="dsa_prep",
    )(proj, q_gain.reshape(1, HEAD_DIM), k_gain.reshape(1, HEAD_DIM))


def _t5_bucket_tiles():
    half = REL_BUCKETS // 2
    max_exact = half // 2
    s = np.arange(BLK)[:, None]
    t = np.arange(BLK)[None, :]
    tiles = []
    for behind in range(3):
        rel = s - t - behind * BLK
        n = np.abs(rel)
        nf = np.maximum(n, max_exact).astype(np.float64)
        large = max_exact + (np.log(nf / max_exact) / math.log(REL_MAX_DIST / max_exact)
                             * (half - max_exact)).astype(np.int32)
        large = np.minimum(large, half - 1)
        tiles.append(np.where(rel > 0, half, 0) + np.where(n < max_exact, n, large))
    return np.stack(tiles).astype(np.int32)


def _rel_bias_tiles_kernel(rb_ref, bucket_ref, o_ref):
    h = pl.program_id(1)
    bucket = bucket_ref[...]
    acc = jnp.zeros((BLK, BLK), F32)
    for b in range(REL_BUCKETS):
        acc = jnp.where(bucket == b, rb_ref[h, b], acc)
    o_ref[...] = acc * LOG2_E


def rel_bias_tiles(rel_bias):
    return pl.pallas_call(
        _rel_bias_tiles_kernel,
        grid=(3, DSA_HEADS),
        in_specs=[pl.BlockSpec(memory_space=pltpu.SMEM),
                  pl.BlockSpec((None, BLK, BLK), lambda d, h: (d, 0, 0))],
        out_specs=pl.BlockSpec((None, BLK, BLK), lambda d, h: (d, 0, h)),
        out_shape=jax.ShapeDtypeStruct((3, BLK, DSA_HEADS * BLK), F32),
        name="rel_bias_tiles",
    )(rel_bias, jnp.asarray(_t5_bucket_tiles()))


def _dsa_attention_kernel(topk, q_ref, k_ref, vt_ref, qi_ref, ki_ref, wit_ref, bias_ref,
                          o_ref, score_ref, sel_ref, acc_ref, qn_ref, qi2_ref):
    tile = pl.program_id(1)
    nch = tile + 1
    qblocks = KC // BLK
    key = lax.broadcasted_iota(jnp.int32, (KC, KC), 0)
    t_local = lax.broadcasted_iota(jnp.int32, (1, KC), 1)
    n_adm = tile * KC + (t_local // CHUNK + 1) * CHUNK
    n_adm_b = jnp.broadcast_to(n_adm, (KC, KC))
    block_end = tile * KC + (t_local // BLK + 1) * BLK
    keep = jnp.minimum(jnp.minimum(topk, block_end), n_adm).astype(F32)

    for p in range(IDX_HEADS // 2):
        qi2_ref[p] = jnp.concatenate(
            [qi_ref[:, (2 * p + e) * IDX_HEAD_DIM:(2 * p + e + 1) * IDX_HEAD_DIM] for e in range(2)], axis=0)
    w_rows = jnp.concatenate([wit_ref[e] for e in range(qblocks)], axis=1)

    def score_chunk(c, _):
        kic = ki_ref[pl.ds(pl.multiple_of(c * KC, KC), KC), :]
        ds = [lax.dot_general(kic, qi2_ref[p], _NT, preferred_element_type=F32)
              for p in range(IDX_HEADS // 2)]
        s = jnp.zeros((KC, KC), F32)
        for p in range(IDX_HEADS // 2):
            s = s + jnp.maximum(ds[p][:, :KC], 0.0) * w_rows[2 * p:2 * p + 1, :]
            s = s + jnp.maximum(ds[p][:, KC:], 0.0) * w_rows[2 * p + 1:2 * p + 2, :]
        score_ref[c] = jnp.where(key + c * KC < n_adm_b, s, -jnp.inf)
        return 0

    lax.fori_loop(0, nch, score_chunk, 0)

    def fold(tile, op):
        return op(tile.reshape(KC // 8, 8, KC), axis=0)

    def count(pred, x):
        xb = jnp.broadcast_to(x, (KC, KC))
        c = lax.fori_loop(0, nch, lambda c, acc: acc + fold(pred(score_ref[c], xb).astype(F32), jnp.sum),
                          jnp.zeros((8, KC), F32))
        return jnp.sum(c, axis=0, keepdims=True)

    def minmax_body(c, st):
        s = score_ref[c]
        return (jnp.minimum(st[0], fold(jnp.where(s == -jnp.inf, jnp.inf, s), jnp.min)),
                jnp.maximum(st[1], fold(s, jnp.max)))

    mn, mx = lax.fori_loop(0, nch, minmax_body,
                           (jnp.full((8, KC), jnp.inf, F32), jnp.full((8, KC), -jnp.inf, F32)))
    lo0 = jnp.min(mn, axis=0, keepdims=True)
    hi0 = jnp.max(mx, axis=0, keepdims=True)

    def midpoint(lo, hi):
        return 0.5 * lo + 0.5 * hi

    def bisect_cond(st):
        it, lo, hi, clo = st
        mid = midpoint(lo, hi)
        open_ = (clo > keep) & (mid > lo) & (mid < hi)
        return (it < MAX_BISECT) & (jnp.max(open_.astype(F32)) > 0.0)

    def bisect_body(st):
        it, lo, hi, clo = st
        for _ in range(BISECT_STEPS_PER_CHECK):
            mid = midpoint(lo, hi)
            c = count(lambda s, x: s >= x, mid)
            ge = c >= keep
            lo, hi, clo = jnp.where(ge, mid, lo), jnp.where(ge, hi, mid), jnp.where(ge, c, clo)
        return it + BISECT_STEPS_PER_CHECK, lo, hi, clo

    _, thr, _, cthr = lax.while_loop(bisect_cond, bisect_body,
                                     (jnp.int32(0), lo0, hi0, n_adm.astype(F32)))
    thr_b = jnp.broadcast_to(thr, (KC, KC))
    tied = jnp.max(cthr - keep) > 0.0

    @pl.when(jnp.logical_not(tied))
    def _():
        def select_chunk(c, _):
            sel_ref[c] = jnp.where(score_ref[c] >= thr_b, 0.0, MASKED)
            return 0

        lax.fori_loop(0, nch, select_chunk, 0)

    @pl.when(tied)
    def _():
        room = jnp.broadcast_to(keep - count(lambda s, x: s > x, thr), (KC, KC))
        earlier = (lax.broadcasted_iota(jnp.int32, (KC, KC), 1) < key).astype(BF16)

        def select_chunk(c, seen):
            s = score_ref[c]
            eq = s == thr_b
            rank = jnp.dot(earlier, eq.astype(BF16), preferred_element_type=F32) + seen
            sel_ref[c] = jnp.where((s > thr_b) | (eq & (rank < room)), 0.0, MASKED)
            return seen + jnp.sum(eq.astype(F32), axis=0, keepdims=True)

        lax.fori_loop(0, nch, select_chunk, jnp.zeros((1, KC), F32))

    cols = DSA_GROUP * KC
    kv_cols = [slice(n * HEAD_DIM, (n + 1) * HEAD_DIM) for n in range(DSA_KV_HEADS)]
    for n in range(DSA_KV_HEADS):
        qn_ref[n] = jnp.concatenate(
            [q_ref[:, (n * DSA_GROUP + g) * HEAD_DIM:(n * DSA_GROUP + g + 1) * HEAD_DIM]
             for g in range(DSA_GROUP)], axis=0)
    acc_ref[...] = jnp.zeros_like(acc_ref)

    def attend(c, st, far):
        kstart = pl.multiple_of(c * KC, KC)
        behind = [[jnp.clip(qblocks * (tile - c) + qb - kb, 0, 2) for qb in range(qblocks)]
                  for kb in range(qblocks)]
        sel = jnp.concatenate([sel_ref[c]] * DSA_GROUP, axis=1)
        raw = [lax.dot_general(k_ref[pl.ds(kstart, KC), kv_cols[n]], qn_ref[n], _NT,
                               preferred_element_type=F32) for n in range(DSA_KV_HEADS)]
        probs, alphas, out = [], [], []
        for n in range(DSA_KV_HEADS):
            m, l = st[2 * n], st[2 * n + 1]
            heads = [slice((n * DSA_GROUP + g) * BLK, (n * DSA_GROUP + g + 1) * BLK) for g in range(DSA_GROUP)]
            s = raw[n] * (HEAD_DIM ** -0.5 * LOG2_E) + sel
            if far:
                bias = jnp.concatenate([bias_ref[2, 0:1, hs] for hs in heads for _ in range(qblocks)], axis=1)
                m_new = jnp.maximum(m, jnp.max(s, axis=0, keepdims=True) + bias)
                p = jnp.exp2(s - (m_new - bias))
            else:
                s = s + jnp.concatenate(
                    [jnp.concatenate([bias_ref[behind[kb][qb], :, hs] for hs in heads for qb in range(qblocks)],
                                     axis=1) for kb in range(qblocks)], axis=0)
                m_new = jnp.maximum(m, jnp.max(s, axis=0, keepdims=True))
                p = jnp.exp2(s - m_new)
            alpha = jnp.exp2(m - m_new)
            probs.append(p.astype(BF16))
            alphas.append(alpha)
            out += [m_new, alpha * l + jnp.sum(p, axis=0, keepdims=True)]
        pvs = [jnp.dot(vt_ref[c, kv_cols[n], :], probs[n], preferred_element_type=F32)
               for n in range(DSA_KV_HEADS)]
        for n in range(DSA_KV_HEADS):
            acc_ref[n] = alphas[n] * acc_ref[n] + pvs[n]
        return tuple(out)

    nfar = jnp.maximum(nch - 2, 0)
    stats = lax.fori_loop(
        0, nfar, functools.partial(attend, far=True),
        (jnp.full((1, cols), MASKED, F32), jnp.zeros((1, cols), F32)) * DSA_KV_HEADS)
    stats = lax.fori_loop(nfar, nch, functools.partial(attend, far=False), stats)
    for n in range(DSA_KV_HEADS):
        o = acc_ref[n] / stats[2 * n + 1]
        for g in range(DSA_GROUP):
            h = n * DSA_GROUP + g
            o_ref[:, h * HEAD_DIM:(h + 1) * HEAD_DIM] = o[:, g * KC:(g + 1) * KC].T.astype(o_ref.dtype)


def dsa_attention(q, k, vt, qi, ki, wit, bias_tiles):
    b, s, _ = q.shape
    nc = s // KC
    topk = min(IDX_TOPK_MAX, s // 4)
    return pl.pallas_call(
        functools.partial(_dsa_attention_kernel, topk),
        grid=(b, nc),
        in_specs=[pl.BlockSpec((None, KC, DSA_Q), lambda bi, i: (bi, i, 0)),
                  pl.BlockSpec((None, s, DSA_KV), lambda bi, i: (bi, 0, 0)),
                  pl.BlockSpec((None, nc, DSA_KV, KC), lambda bi, i: (bi, 0, 0, 0)),
                  pl.BlockSpec((None, KC, DSA_QI), lambda bi, i: (bi, i, 0)),
                  pl.BlockSpec((None, s, IDX_HEAD_DIM), lambda bi, i: (bi, 0, 0)),
                  pl.BlockSpec((None, KC // BLK, IDX_HEADS, BLK), lambda bi, i: (bi, i, 0, 0)),
                  pl.BlockSpec((3, BLK, DSA_HEADS * BLK), lambda bi, i: (0, 0, 0))],
        out_specs=pl.BlockSpec((None, KC, DSA_Q), lambda bi, i: (bi, i, 0)),
        out_shape=jax.ShapeDtypeStruct((b, s, DSA_Q), BF16),
        scratch_shapes=[pltpu.VMEM((nc, KC, KC), F32), pltpu.VMEM((nc, KC, KC), F32),
                        pltpu.VMEM((DSA_KV_HEADS, HEAD_DIM, DSA_GROUP * KC), F32),
                        pltpu.VMEM((DSA_KV_HEADS, DSA_GROUP * KC, HEAD_DIM), BF16),
                        pltpu.VMEM((IDX_HEADS // 2, 2 * KC, IDX_HEAD_DIM), BF16)],
        compiler_params=pltpu.CompilerParams(dimension_semantics=("parallel", "arbitrary")),
        name="dsa_attention",
    )(q, k, vt, qi, ki, wit, bias_tiles)


def kernel(x, norm_mix, w_in_a, w_out_a, w_in_b, w_out_b, q_norm_b, k_norm_b, rel_bias,
           norm_mlp, w_up, w_down):
    b, s, d = x.shape
    n = b * s
    nb = s // BLK
    depth = norm_mix.shape[0]
    w_in_b = jnp.pad(w_in_b, ((0, 0), (0, 0), (0, DSA_IN_PADDED - DSA_IN)))
    w_in_a, w_out_a, w_in_b, w_out_b, w_up, w_down = (
        w.astype(BF16) for w in (w_in_a, w_out_a, w_in_b, w_out_b, w_up, w_down))
    bias_tiles = rel_bias_tiles(rel_bias)

    x = x.reshape(n, d)
    for layer in range(depth):
        j = layer // 2
        if layer % 2 == 0:
            qkv = norm_matmul(x, norm_mix[layer], w_in_a, j, BF16)
            o = stick_breaking_attention(qkv.reshape(b, s, -1))
            x = matmul_residual(o.reshape(n, -1), w_out_a, j, x)
        else:
            proj = norm_matmul(x, norm_mix[layer], w_in_b, j, F32, tn=DSA_IN_PADDED // 3)
            q, k, vt, qi, ki, wit = dsa_prep(proj, q_norm_b[j], k_norm_b[j])
            o = dsa_attention(q.reshape(b, s, -1), k.reshape(b, s, -1),
                              vt.reshape(b, s // KC, DSA_KV, KC), qi.reshape(b, s, -1),
                              ki.reshape(b, s, -1), wit.reshape(b, nb, IDX_HEADS, BLK), bias_tiles)
            x = matmul_residual(o.reshape(n, -1), w_out_b, j, x)
        x = mlp_residual(x, norm_mlp[layer], w_up, w_down, layer)
    return x.reshape(b, s, d)
```

```python
import functools
import math

import jax
import jax.numpy as jnp
import numpy as np
from jax import lax
from jax.experimental import pallas as pl
from jax.experimental.pallas import tpu as pltpu

F32 = jnp.float32
BF16 = jnp.bfloat16

RMS_EPS = 1e-6
NORM_STRIP = 128
CHUNK = 64
BLK = 128
KC = 2 * BLK
HEAD_DIM = 128
SB_HEADS = 16
SB_QBLOCKS = 4
DSA_HEADS = 16
DSA_KV_HEADS = 4
DSA_GROUP = DSA_HEADS // DSA_KV_HEADS
IDX_HEADS = 16
IDX_HEAD_DIM = 64
IDX_TOPK_MAX = 256
REL_BUCKETS = 32
REL_MAX_DIST = 128
MASKED = -1e30
LOG2_E = math.log2(math.e)
MAX_BISECT = 320
BISECT_STEPS_PER_CHECK = 4
EXP2_IS_ZERO_BELOW = -150.0

_NT = (((1,), (1,)), ((), ()))


def _rms_rows(x, g):
    return x * lax.rsqrt(jnp.mean(x * x, axis=-1, keepdims=True) + RMS_EPS) * g


def _normalize_rows_into(x_ref, g_ref, xn_ref):
    def strip(r, _):
        rows = pl.ds(pl.multiple_of(r * NORM_STRIP, NORM_STRIP), NORM_STRIP)
        xn_ref[rows, :] = _rms_rows(x_ref[rows, :], g_ref[...]).astype(BF16)
        return 0

    lax.fori_loop(0, x_ref.shape[0] // NORM_STRIP, strip, 0)


def _norm_matmul_kernel(x_ref, g_ref, w_ref, o_ref, xn_ref):
    @pl.when(pl.program_id(1) == 0)
    def _():
        _normalize_rows_into(x_ref, g_ref, xn_ref)

    o_ref[...] = jnp.dot(xn_ref[...], w_ref[...], preferred_element_type=F32).astype(o_ref.dtype)


def norm_matmul(x, g, w, layer, out_dtype, tm=1024, tn=1024):
    n, d = x.shape
    m = w.shape[2]
    return pl.pallas_call(
        _norm_matmul_kernel,
        grid=(n // tm, m // tn),
        in_specs=[pl.BlockSpec((tm, d), lambda i, j: (i, 0)),
                  pl.BlockSpec((1, d), lambda i, j: (0, 0)),
                  pl.BlockSpec((None, d, tn), lambda i, j: (layer, 0, j))],
        out_specs=pl.BlockSpec((tm, tn), lambda i, j: (i, j)),
        out_shape=jax.ShapeDtypeStruct((n, m), out_dtype),
        scratch_shapes=[pltpu.VMEM((tm, d), BF16)],
        compiler_params=pltpu.CompilerParams(dimension_semantics=("parallel", "arbitrary")),
        name="norm_matmul",
    )(x, g.reshape(1, d), w)


def _matmul_residual_kernel(a_ref, w_ref, r_ref, o_ref):
    o_ref[...] = r_ref[...] + jnp.dot(a_ref[...], w_ref[...], preferred_element_type=F32)


def matmul_residual(a, w, layer, res, tm=512):
    n, k = a.shape
    m = w.shape[2]
    return pl.pallas_call(
        _matmul_residual_kernel,
        grid=(n // tm,),
        in_specs=[pl.BlockSpec((tm, k), lambda i: (i, 0)),
                  pl.BlockSpec((None, k, m), lambda i: (layer, 0, 0)),
                  pl.BlockSpec((tm, m), lambda i: (i, 0))],
        out_specs=pl.BlockSpec((tm, m), lambda i: (i, 0)),
        out_shape=jax.ShapeDtypeStruct((n, m), F32),
        compiler_params=pltpu.CompilerParams(dimension_semantics=("parallel",)),
        name="matmul_residual",
    )(a, w, res)


def _mlp_kernel(x_ref, g_ref, wu_ref, wd_ref, o_ref, xn_ref):
    f = pl.program_id(1)

    @pl.when(f == 0)
    def _():
        _normalize_rows_into(x_ref, g_ref, xn_ref)
        o_ref[...] = x_ref[...]

    h = jnp.dot(xn_ref[...], wu_ref[...], preferred_element_type=F32)
    a = jnp.square(jnp.maximum(h, 0.0)).astype(BF16)
    o_ref[...] += jnp.dot(a, wd_ref[...], preferred_element_type=F32)


def mlp_residual(x, g, w_up, w_down, layer, tm=512, tf=1024):
    n, d = x.shape
    dff = w_up.shape[2]
    return pl.pallas_call(
        _mlp_kernel,
        grid=(n // tm, dff // tf),
        in_specs=[pl.BlockSpec((tm, d), lambda i, f: (i, 0)),
                  pl.BlockSpec((1, d), lambda i, f: (0, 0)),
                  pl.BlockSpec((None, d, tf), lambda i, f: (layer, 0, f)),
                  pl.BlockSpec((None, tf, d), lambda i, f: (layer, f, 0))],
        out_specs=pl.BlockSpec((tm, d), lambda i, f: (i, 0)),
        out_shape=jax.ShapeDtypeStruct((n, d), F32),
        scratch_shapes=[pltpu.VMEM((tm, d), BF16)],
        compiler_params=pltpu.CompilerParams(dimension_semantics=("parallel", "arbitrary")),
        name="mlp_residual",
    )(x, g.reshape(1, d), w_up, w_down)


def _stick_breaking_kernel(q_ref, k_ref, v_ref, o_ref, acc_ref):
    tile = pl.program_id(2)
    heads = q_ref.shape[1] // HEAD_DIM
    scale = HEAD_DIM ** -0.5 * LOG2_E
    row = lax.broadcasted_iota(jnp.int32, (BLK, BLK), 0)
    col = lax.broadcasted_iota(jnp.int32, (BLK, BLK), 1)
    later = (row > col).astype(BF16)
    later2 = jnp.concatenate([later, later], axis=0)
    causal = col < row
    chains = [(slice(qb * BLK, (qb + 1) * BLK), slice(g * HEAD_DIM, (g + 1) * HEAD_DIM))
              for qb in range(SB_QBLOCKS) for g in range(heads)]

    def step(back, carries, diagonal):
        first = tile * SB_QBLOCKS - back
        if not diagonal:
            carries = [jnp.where(first + n // heads >= 0, c, MASKED) for n, c in enumerate(carries)]
        starts = [pl.multiple_of(jnp.maximum(first + qb, 0) * BLK, BLK) for qb in range(SB_QBLOCKS)]
        zs = [lax.dot_general(q_ref[rows, cs], k_ref[pl.ds(starts[n // heads], BLK), cs], _NT,
                              preferred_element_type=F32) for n, (rows, cs) in enumerate(chains)]
        log_bs, log_1mbs, splits = [], [], []
        for z in zs:
            z = z * scale
            log_b = jnp.minimum(z, 0.0) - jnp.log2(1.0 + jnp.exp2(-jnp.abs(z)))
            log_1mb = log_b - z
            if diagonal:
                log_1mb = jnp.where(causal, log_1mb, 0.0)
            hi = log_1mb.astype(BF16)
            lo = (log_1mb - hi.astype(F32)).astype(BF16)
            log_bs.append(log_b)
            log_1mbs.append(log_1mb)
            splits.append(jnp.concatenate([hi, lo], axis=1))
        tails = jnp.dot(jnp.concatenate(splits, axis=0), later2, preferred_element_type=F32)
        weights = []
        for n in range(len(chains)):
            a = jnp.exp2(log_bs[n] + tails[n * BLK:(n + 1) * BLK] + carries[n])
            if diagonal:
                a = jnp.where(causal, a, 0.0)
            weights.append(a.astype(BF16))
        contribs = [jnp.dot(weights[n], v_ref[pl.ds(starts[n // heads], BLK), cs],
                            preferred_element_type=F32) for n, (_, cs) in enumerate(chains)]
        for n, (rows, cs) in enumerate(chains):
            if diagonal:
                acc_ref[rows, cs] = contribs[n]
            else:
                acc_ref[rows, cs] += contribs[n]
        return tuple(carries[n] + jnp.sum(log_1mbs[n], axis=1, keepdims=True) for n in range(len(chains)))

    carries = step(0, (jnp.zeros((BLK, 1), F32),) * len(chains), True)

    def cond(state):
        live = functools.reduce(jnp.maximum, state[1:])
        return (state[0] <= tile * SB_QBLOCKS + SB_QBLOCKS - 1) & (jnp.max(live) > EXP2_IS_ZERO_BELOW)

    def body(state):
        return (state[0] + 1,) + step(state[0], list(state[1:]), False)

    lax.while_loop(cond, body, (jnp.int32(1),) + carries)
    o_ref[...] = acc_ref[...].astype(o_ref.dtype)


def stick_breaking_attention(qkv, heads_per_step=8):
    b, s, _ = qkv.shape
    groups = SB_HEADS // heads_per_step
    w = heads_per_step * HEAD_DIM
    tq = SB_QBLOCKS * BLK
    return pl.pallas_call(
        _stick_breaking_kernel,
        grid=(b, groups, s // tq),
        in_specs=[pl.BlockSpec((None, tq, w), lambda bi, gi, i: (bi, i, gi)),
                  pl.BlockSpec((None, s, w), lambda bi, gi, i: (bi, 0, groups + gi)),
                  pl.BlockSpec((None, s, w), lambda bi, gi, i: (bi, 0, 2 * groups + gi))],
        out_specs=pl.BlockSpec((None, tq, w), lambda bi, gi, i: (bi, i, gi)),
        out_shape=jax.ShapeDtypeStruct((b, s, SB_HEADS * HEAD_DIM), BF16),
        scratch_shapes=[pltpu.VMEM((tq, w), F32)],
        compiler_params=pltpu.CompilerParams(
            dimension_semantics=("parallel", "parallel", "arbitrary")),
        name="stick_breaking_attention",
    )(qkv, qkv, qkv)


DSA_Q = DSA_HEADS * HEAD_DIM
DSA_KV = DSA_KV_HEADS * HEAD_DIM
DSA_QI = IDX_HEADS * IDX_HEAD_DIM
DSA_SPLITS = (0, DSA_Q, DSA_Q + DSA_KV, DSA_Q + 2 * DSA_KV, DSA_Q + 2 * DSA_KV + DSA_QI,
              DSA_Q + 2 * DSA_KV + DSA_QI + IDX_HEAD_DIM,
              DSA_Q + 2 * DSA_KV + DSA_QI + IDX_HEAD_DIM + IDX_HEADS)
DSA_IN = DSA_SPLITS[-1]
DSA_IN_PADDED = -(-DSA_IN // 128) * 128
assert DSA_SPLITS[4] % 128 == 0 and DSA_IN_PADDED - DSA_SPLITS[4] == 128 and IDX_HEAD_DIM % 8 == 0


def _dsa_prep_kernel(p_ref, qg_ref, kg_ref, q_ref, k_ref, vt_ref, qi_ref, ki_ref, wit_ref):
    s = DSA_SPLITS
    for h in range(DSA_HEADS):
        lo = s[0] + h * HEAD_DIM
        q_ref[:, h * HEAD_DIM:(h + 1) * HEAD_DIM] = _rms_rows(
            p_ref[:, lo:lo + HEAD_DIM], qg_ref[...]).astype(BF16)
    for h in range(DSA_KV_HEADS):
        lo = s[1] + h * HEAD_DIM
        k_ref[:, h * HEAD_DIM:(h + 1) * HEAD_DIM] = _rms_rows(
            p_ref[:, lo:lo + HEAD_DIM], kg_ref[...]).astype(BF16)
        lo = s[2] + h * HEAD_DIM
        vt_ref[h * HEAD_DIM:(h + 1) * HEAD_DIM, :] = p_ref[:, lo:lo + HEAD_DIM].T.astype(BF16)
    qi_ref[...] = p_ref[:, s[3]:s[4]].astype(BF16)
    ki_ref[...] = p_ref[:, s[4]:s[5]].astype(BF16)
    tail_t = p_ref[:, s[4]:DSA_IN_PADDED].T
    w_t = tail_t[IDX_HEAD_DIM:IDX_HEAD_DIM + IDX_HEADS, :] * (IDX_HEADS ** -0.5 * IDX_HEAD_DIM ** -0.5)
    for e in range(KC // BLK):
        wit_ref[e] = w_t[:, e * BLK:(e + 1) * BLK]


def dsa_prep(proj, q_gain, k_gain):
    n = proj.shape[0]
    nc = n // KC
    row = lambda w: pl.BlockSpec((KC, w), lambda i: (i, 0))
    return pl.pallas_call(
        _dsa_prep_kernel,
        grid=(nc,),
        in_specs=[row(DSA_IN_PADDED),
                  pl.BlockSpec((1, HEAD_DIM), lambda i: (0, 0)),
                  pl.BlockSpec((1, HEAD_DIM), lambda i: (0, 0))],
        out_specs=[row(DSA_Q), row(DSA_KV),
                   pl.BlockSpec((None, DSA_KV, KC), lambda i: (i, 0, 0)),
                   row(DSA_QI), row(IDX_HEAD_DIM),
                   pl.BlockSpec((KC // BLK, IDX_HEADS, BLK), lambda i: (i, 0, 0))],
        out_shape=[jax.ShapeDtypeStruct((n, DSA_Q), BF16),
                   jax.ShapeDtypeStruct((n, DSA_KV), BF16),
                   jax.ShapeDtypeStruct((nc, DSA_KV, KC), BF16),
                   jax.ShapeDtypeStruct((n, DSA_QI), BF16),
                   jax.ShapeDtypeStruct((n, IDX_HEAD_DIM), BF16),
                   jax.ShapeDtypeStruct((n // BLK, IDX_HEADS, BLK), F32)],
        compiler_params=pltpu.CompilerParams(dimension_semantics=("parallel",)),
        name="dsa_prep",
    )(proj, q_gain.reshape(1, HEAD_DIM), k_gain.reshape(1, HEAD_DIM))


def _t5_bucket_tiles():
    half = REL_BUCKETS // 2
    max_exact = half // 2
    s = np.arange(BLK)[:, None]
    t = np.arange(BLK)[None, :]
    tiles = []
    for behind in range(3):
        rel = s - t - behind * BLK
        n = np.abs(rel)
        nf = np.maximum(n, max_exact).astype(np.float64)
        large = max_exact + (np.log(nf / max_exact) / math.log(REL_MAX_DIST / max_exact)
                             * (half - max_exact)).astype(np.int32)
        large = np.minimum(large, half - 1)
        tiles.append(np.where(rel > 0, half, 0) + np.where(n < max_exact, n, large))
    return np.stack(tiles).astype(np.int32)


def _rel_bias_tiles_kernel(rb_ref, bucket_ref, o_ref):
    h = pl.program_id(1)
    bucket = bucket_ref[...]
    acc = jnp.zeros((BLK, BLK), F32)
    for b in range(REL_BUCKETS):
        acc = jnp.where(bucket == b, rb_ref[h, b], acc)
    o_ref[...] = acc * LOG2_E


def rel_bias_tiles(rel_bias):
    return pl.pallas_call(
        _rel_bias_tiles_kernel,
        grid=(3, DSA_HEADS),
        in_specs=[pl.BlockSpec(memory_space=pltpu.SMEM),
                  pl.BlockSpec((None, BLK, BLK), lambda d, h: (d, 0, 0))],
        out_specs=pl.BlockSpec((None, BLK, BLK), lambda d, h: (d, 0, h)),
        out_shape=jax.ShapeDtypeStruct((3, BLK, DSA_HEADS * BLK), F32),
        name="rel_bias_tiles",
    )(rel_bias, jnp.asarray(_t5_bucket_tiles()))


def _dsa_attention_kernel(topk, q_ref, k_ref, vt_ref, qi_ref, ki_ref, wit_ref, bias_ref,
                          o_ref, score_ref, sel_ref, acc_ref, qn_ref, qi2_ref):
    tile = pl.program_id(1)
    nch = tile + 1
    qblocks = KC // BLK
    key = lax.broadcasted_iota(jnp.int32, (KC, KC), 0)
    t_local = lax.broadcasted_iota(jnp.int32, (1, KC), 1)
    n_adm = tile * KC + (t_local // CHUNK + 1) * CHUNK
    n_adm_b = jnp.broadcast_to(n_adm, (KC, KC))
    block_end = tile * KC + (t_local // BLK + 1) * BLK
    keep = jnp.minimum(jnp.minimum(topk, block_end), n_adm).astype(F32)

    for p in range(IDX_HEADS // 2):
        qi2_ref[p] = jnp.concatenate(
            [qi_ref[:, (2 * p + e) * IDX_HEAD_DIM:(2 * p + e + 1) * IDX_HEAD_DIM] for e in range(2)], axis=0)
    w_rows = jnp.concatenate([wit_ref[e] for e in range(qblocks)], axis=1)

    def score_chunk(c, _):
        kic = ki_ref[pl.ds(pl.multiple_of(c * KC, KC), KC), :]
        ds = [lax.dot_general(kic, qi2_ref[p], _NT, preferred_element_type=F32)
              for p in range(IDX_HEADS // 2)]
        s = jnp.zeros((KC, KC), F32)
        for p in range(IDX_HEADS // 2):
            s = s + jnp.maximum(ds[p][:, :KC], 0.0) * w_rows[2 * p:2 * p + 1, :]
            s = s + jnp.maximum(ds[p][:, KC:], 0.0) * w_rows[2 * p + 1:2 * p + 2, :]
        score_ref[c] = jnp.where(key + c * KC < n_adm_b, s, -jnp.inf)
        return 0

    lax.fori_loop(0, nch, score_chunk, 0)

    def fold(tile, op):
        return op(tile.reshape(KC // 8, 8, KC), axis=0)

    def count(pred, x):
        xb = jnp.broadcast_to(x, (KC, KC))
        c = lax.fori_loop(0, nch, lambda c, acc: acc + fold(pred(score_ref[c], xb).astype(F32), jnp.sum),
                          jnp.zeros((8, KC), F32))
        return jnp.sum(c, axis=0, keepdims=True)

    def minmax_body(c, st):
        s = score_ref[c]
        return (jnp.minimum(st[0], fold(jnp.where(s == -jnp.inf, jnp.inf, s), jnp.min)),
                jnp.maximum(st[1], fold(s, jnp.max)))

    mn, mx = lax.fori_loop(0, nch, minmax_body,
                           (jnp.full((8, KC), jnp.inf, F32), jnp.full((8, KC), -jnp.inf, F32)))
    lo0 = jnp.min(mn, axis=0, keepdims=True)
    hi0 = jnp.max(mx, axis=0, keepdims=True)

    def midpoint(lo, hi):
        return 0.5 * lo + 0.5 * hi

    def bisect_cond(st):
        it, lo, hi, clo = st
        mid = midpoint(lo, hi)
        open_ = (clo > keep) & (mid > lo) & (mid < hi)
        return (it < MAX_BISECT) & (jnp.max(open_.astype(F32)) > 0.0)

    def bisect_body(st):
        it, lo, hi, clo = st
        for _ in range(BISECT_STEPS_PER_CHECK):
            mid = midpoint(lo, hi)
            c = count(lambda s, x: s >= x, mid)
            ge = c >= keep
            lo, hi, clo = jnp.where(ge, mid, lo), jnp.where(ge, hi, mid), jnp.where(ge, c, clo)
        return it + BISECT_STEPS_PER_CHECK, lo, hi, clo

    _, thr, _, cthr = lax.while_loop(bisect_cond, bisect_body,
                                     (jnp.int32(0), lo0, hi0, n_adm.astype(F32)))
    thr_b = jnp.broadcast_to(thr, (KC, KC))
    tied = jnp.max(cthr - keep) > 0.0

    @pl.when(jnp.logical_not(tied))
    def _():
        def select_chunk(c, _):
            sel_ref[c] = jnp.where(score_ref[c] >= thr_b, 0.0, MASKED)
            return 0

        lax.fori_loop(0, nch, select_chunk, 0)

    @pl.when(tied)
    def _():
        room = jnp.broadcast_to(keep - count(lambda s, x: s > x, thr), (KC, KC))
        earlier = (lax.broadcasted_iota(jnp.int32, (KC, KC), 1) < key).astype(BF16)

        def select_chunk(c, seen):
            s = score_ref[c]
            eq = s == thr_b
            rank = jnp.dot(earlier, eq.astype(BF16), preferred_element_type=F32) + seen
            sel_ref[c] = jnp.where((s > thr_b) | (eq & (rank < room)), 0.0, MASKED)
            return seen + jnp.sum(eq.astype(F32), axis=0, keepdims=True)

        lax.fori_loop(0, nch, select_chunk, jnp.zeros((1, KC), F32))

    cols = DSA_GROUP * KC
    kv_cols = [slice(n * HEAD_DIM, (n + 1) * HEAD_DIM) for n in range(DSA_KV_HEADS)]
    for n in range(DSA_KV_HEADS):
        qn_ref[n] = jnp.concatenate(
            [q_ref[:, (n * DSA_GROUP + g) * HEAD_DIM:(n * DSA_GROUP + g + 1) * HEAD_DIM]
             for g in range(DSA_GROUP)], axis=0)
    acc_ref[...] = jnp.zeros_like(acc_ref)

    def attend(c, st):
        kstart = pl.multiple_of(c * KC, KC)
        behind = [[jnp.clip(qblocks * (tile - c) + qb - kb, 0, 2) for qb in range(qblocks)]
                  for kb in range(qblocks)]
        sel = jnp.concatenate([sel_ref[c]] * DSA_GROUP, axis=1)
        raw = [lax.dot_general(k_ref[pl.ds(kstart, KC), kv_cols[n]], qn_ref[n], _NT,
                               preferred_element_type=F32) for n in range(DSA_KV_HEADS)]
        probs, alphas, out = [], [], []
        for n in range(DSA_KV_HEADS):
            m, l = st[2 * n], st[2 * n + 1]
            heads = [slice((n * DSA_GROUP + g) * BLK, (n * DSA_GROUP + g + 1) * BLK) for g in range(DSA_GROUP)]
            bias = jnp.concatenate(
                [jnp.concatenate([bias_ref[behind[kb][qb], :, hs] for hs in heads for qb in range(qblocks)],
                                 axis=1) for kb in range(qblocks)], axis=0)
            s = raw[n] * (HEAD_DIM ** -0.5 * LOG2_E) + bias + sel
            m_new = jnp.maximum(m, jnp.max(s, axis=0, keepdims=True))
            alpha = jnp.exp2(m - m_new)
            p = jnp.exp2(s - m_new)
            probs.append(p.astype(BF16))
            alphas.append(alpha)
            out += [m_new, alpha * l + jnp.sum(p, axis=0, keepdims=True)]
        pvs = [jnp.dot(vt_ref[c, kv_cols[n], :], probs[n], preferred_element_type=F32)
               for n in range(DSA_KV_HEADS)]
        for n in range(DSA_KV_HEADS):
            acc_ref[n] = alphas[n] * acc_ref[n] + pvs[n]
        return tuple(out)

    stats = lax.fori_loop(
        0, nch, attend,
        (jnp.full((1, cols), MASKED, F32), jnp.zeros((1, cols), F32)) * DSA_KV_HEADS)
    for n in range(DSA_KV_HEADS):
        o = acc_ref[n] / stats[2 * n + 1]
        for g in range(DSA_GROUP):
            h = n * DSA_GROUP + g
            o_ref[:, h * HEAD_DIM:(h + 1) * HEAD_DIM] = o[:, g * KC:(g + 1) * KC].T.astype(o_ref.dtype)


def dsa_attention(q, k, vt, qi, ki, wit, bias_tiles):
    b, s, _ = q.shape
    nc = s // KC
    topk = min(IDX_TOPK_MAX, s // 4)
    return pl.pallas_call(
        functools.partial(_dsa_attention_kernel, topk),
        grid=(b, nc),
        in_specs=[pl.BlockSpec((None, KC, DSA_Q), lambda bi, i: (bi, i, 0)),
                  pl.BlockSpec((None, s, DSA_KV), lambda bi, i: (bi, 0, 0)),
                  pl.BlockSpec((None, nc, DSA_KV, KC), lambda bi, i: (bi, 0, 0, 0)),
                  pl.BlockSpec((None, KC, DSA_QI), lambda bi, i: (bi, i, 0)),
                  pl.BlockSpec((None, s, IDX_HEAD_DIM), lambda bi, i: (bi, 0, 0)),
                  pl.BlockSpec((None, KC // BLK, IDX_HEADS, BLK), lambda bi, i: (bi, i, 0, 0)),
                  pl.BlockSpec((3, BLK, DSA_HEADS * BLK), lambda bi, i: (0, 0, 0))],
        out_specs=pl.BlockSpec((None, KC, DSA_Q), lambda bi, i: (bi, i, 0)),
        out_shape=jax.ShapeDtypeStruct((b, s, DSA_Q), BF16),
        scratch_shapes=[pltpu.VMEM((nc, KC, KC), F32), pltpu.VMEM((nc, KC, KC), F32),
                        pltpu.VMEM((DSA_KV_HEADS, HEAD_DIM, DSA_GROUP * KC), F32),
                        pltpu.VMEM((DSA_KV_HEADS, DSA_GROUP * KC, HEAD_DIM), BF16),
                        pltpu.VMEM((IDX_HEADS // 2, 2 * KC, IDX_HEAD_DIM), BF16)],
        compiler_params=pltpu.CompilerParams(dimension_semantics=("parallel", "arbitrary")),
        name="dsa_attention",
    )(q, k, vt, qi, ki, wit, bias_tiles)


def kernel(x, norm_mix, w_in_a, w_out_a, w_in_b, w_out_b, q_norm_b, k_norm_b, rel_bias,
           norm_mlp, w_up, w_down):
    b, s, d = x.shape
    n = b * s
    nb = s // BLK
    depth = norm_mix.shape[0]
    w_in_b = jnp.pad(w_in_b, ((0, 0), (0, 0), (0, DSA_IN_PADDED - DSA_IN)))
    w_in_a, w_out_a, w_in_b, w_out_b, w_up, w_down = (
        w.astype(BF16) for w in (w_in_a, w_out_a, w_in_b, w_out_b, w_up, w_down))
    bias_tiles = rel_bias_tiles(rel_bias)

    x = x.reshape(n, d)
    for layer in range(depth):
        j = layer // 2
        if layer % 2 == 0:
            qkv = norm_matmul(x, norm_mix[layer], w_in_a, j, BF16)
            o = stick_breaking_attention(qkv.reshape(b, s, -1))
            x = matmul_residual(o.reshape(n, -1), w_out_a, j, x)
        else:
            proj = norm_matmul(x, norm_mix[layer], w_in_b, j, F32, tn=DSA_IN_PADDED // 3)
            q, k, vt, qi, ki, wit = dsa_prep(proj, q_norm_b[j], k_norm_b[j])
            o = dsa_attention(q.reshape(b, s, -1), k.reshape(b, s, -1),
                              vt.reshape(b, s // KC, DSA_KV, KC), qi.reshape(b, s, -1),
                              ki.reshape(b, s, -1), wit.reshape(b, nb, IDX_HEADS, BLK), bias_tiles)
            x = matmul_residual(o.reshape(n, -1), w_out_b, j, x)
        x = mlp_residual(x, norm_mlp[layer], w_up, w_down, layer)
    return x.reshape(b, s, d)
```
